```python
import math
import jax
import jax.numpy as jnp
from jax import lax
import numpy as np

D_MODEL = 4096
BATCH = 2
SEQ = 8192
DEPTH = 2

GRID_W = 64
CTX_LEN = 256
N_MIXERS = 2
D_FF = 11008
FFN_CONV_K = 3
HY_ORDER = 2
HY_EMB_DIM = 33
HY_FILTER_HIDDEN = 64
HY_SHORT_K = 3
HY_DECAY_TARGET = 1e-2
HY_FAST_DECAY_PCT = 0.3
HY_SLOW_DECAY_PCT = 1.5
GDN_HEADS = 32
GDN_HEAD_DIM = D_MODEL // GDN_HEADS
GDN_CONV_K = 3
GDN_CHUNK = 64
LN_EPS = 1e-5
RMS_EPS = 1e-6
L2_EPS = 1e-6
DN_ALPHA = (2 * DEPTH) ** 0.25
DN_BETA = (8 * DEPTH) ** -0.25

kernel_name = 'hyena_gdn_convglu_hybrid_dit'


def layer_norm(x, g, b):
    xf = x.astype(jnp.float32)
    mu = jnp.mean(xf, axis=-1, keepdims=True)
    var = jnp.mean(jnp.square(xf - mu), axis=-1, keepdims=True)
    return ((xf - mu) * lax.rsqrt(var + LN_EPS) * g + b).astype(x.dtype)


def l2norm(t):
    return t * lax.rsqrt(jnp.sum(t * t, axis=-1, keepdims=True) + L2_EPS)


def adaln(cond, w, b):
    m = (jax.nn.silu(cond) @ w + b)[..., None, :]
    return jnp.split(m, 6, axis=-1)


def modulate(x, shift, scale):
    return x * (1 + scale) + shift


def post_norm(x, y, gate, g, b):
    return layer_norm(DN_ALPHA * x + gate * y, g, b)


def dwconv1d(x, w):
    return lax.conv_general_dilated(x, w[:, None, :].astype(x.dtype), window_strides=(1,), padding='SAME',
                                    dimension_numbers=('NWC', 'WIO', 'NWC'), feature_group_count=x.shape[-1])


def dwconv2d_grid(x, w, b, rows, cols):
    bsz, n_tok, ch = x.shape
    img = x.reshape(bsz, rows, cols, ch)
    y = lax.conv_general_dilated(img, w[:, :, None, :].astype(x.dtype), window_strides=(1, 1), padding='SAME',
                                 dimension_numbers=('NHWC', 'HWIO', 'NHWC'), feature_group_count=ch)
    return y.reshape(bsz, n_tok, ch) + b


def hyena_filters(L, w1, b1, w2, b2, w3, b3, w_out, freq):
    t = jnp.linspace(0.0, 1.0, L, dtype=jnp.float32)[:, None]
    bands = (HY_EMB_DIM - 1) // 2
    ang = 2.0 * math.pi * jnp.arange(L, dtype=jnp.float32)[:, None] / L
    f = jnp.linspace(1e-4, bands - 1, bands, dtype=jnp.float32)[None, :]
    z = jnp.concatenate([t, jnp.cos(f * ang), -jnp.sin(f * ang)], axis=-1)
    h = jnp.sin(freq * (z @ w1 + b1))
    h = jnp.sin(freq * (h @ w2 + b2))
    h = jnp.sin(freq * (h @ w3 + b3))
    h = (h @ w_out).astype(jnp.float32).reshape(L, 2 * HY_ORDER, D_MODEL)
    max_decay = math.log(HY_DECAY_TARGET) / HY_FAST_DECAY_PCT
    min_decay = math.log(HY_DECAY_TARGET) / HY_SLOW_DECAY_PCT
    deltas = jnp.abs(jnp.linspace(min_decay, max_decay, D_MODEL, dtype=jnp.float32))
    return h * jnp.exp(-t[:, :, None] * deltas)


def two_sided_fftconv(z, h_fwd, h_bwd, skip):
    L = z.shape[1]
    k = jnp.concatenate([h_fwd[:1] + h_bwd[:1], h_fwd[1:], jnp.zeros_like(h_fwd[:1]), h_bwd[:0:-1]], axis=0)
    zf = z.astype(jnp.float32)
    zk = jnp.fft.rfft(zf, n=2 * L, axis=1) * jnp.fft.rfft(k, n=2 * L, axis=0)[None]
    y = jnp.fft.irfft(zk, n=2 * L, axis=1)[:, :L]
    return (y + zf * skip).astype(z.dtype)


def hyena_mixer(h, p):
    L = h.shape[1]
    u = dwconv1d(h @ p['hy_w_in'] + p['hy_b_in'], p['hy_w_short']) + p['hy_b_short']
    parts = jnp.split(u, HY_ORDER + 1, axis=-1)
    filt = hyena_filters(L, p['hy_f_w1'], p['hy_f_b1'], p['hy_f_w2'], p['hy_f_b2'], p['hy_f_w3'],
                         p['hy_f_b3'], p['hy_f_wout'], p['hy_f_freq'])
    z = parts[-1]
    for o in range(HY_ORDER):
        z = parts[o] * two_sided_fftconv(z, filt[:, 2 * o], filt[:, 2 * o + 1], p['hy_skip'][o])
    return z @ p['hy_w_out'] + p['hy_b_out']


def gdn_chunked(q, k, v, g, beta, s0):
    bsz, nh, L, dk = q.shape
    dv = v.shape[-1]
    C = GDN_CHUNK
    n = L // C
    q, k, v = (t.reshape(bsz, nh, n, C, t.shape[-1]) for t in (q, k, v))
    g = jnp.cumsum(g.reshape(bsz, nh, n, C), axis=-1)
    beta = beta.reshape(bsz, nh, n, C)
    idx = jnp.arange(C)
    lower_incl = idx[:, None] >= idx[None, :]
    lower_strict = idx[:, None] > idx[None, :]
    decay = jnp.exp(jnp.where(lower_incl, g[..., :, None] - g[..., None, :], -jnp.inf))
    kb = k * beta[..., None]
    a_mat = jnp.where(lower_strict, jnp.einsum('bhnik,bhnjk->bhnij', kb, k) * decay, 0.0)
    rhs = jnp.concatenate([v * beta[..., None], kb * jnp.exp(g)[..., None]], axis=-1)
    sol = lax.linalg.triangular_solve(a_mat + jnp.eye(C, dtype=a_mat.dtype), rhs, left_side=True,
                                      lower=True, unit_diagonal=True)
    u, w = sol[..., :dv], sol[..., dv:]
    attn = jnp.where(lower_incl, jnp.einsum('bhnik,bhnjk->bhnij', q, k) * decay, 0.0)
    qg = q * jnp.exp(g)[..., None]
    kg = k * jnp.exp(g[..., -1:] - g)[..., None]
    g_last = jnp.exp(g[..., -1])

    def step(S, xs):
        qg_i, kg_i, u_i, w_i, attn_i, gl_i = xs
        v_new = u_i - jnp.einsum('bhck,bhkv->bhcv', w_i, S)
        o_i = jnp.einsum('bhck,bhkv->bhcv', qg_i, S) + jnp.einsum('bhij,bhjv->bhiv', attn_i, v_new)
        S = S * gl_i[..., None, None] + jnp.einsum('bhck,bhcv->bhkv', kg_i, v_new)
        return S, o_i

    xs = tuple(jnp.moveaxis(t, 2, 0) for t in (qg, kg, u, w, attn, g_last))
    s_last, o = lax.scan(step, s0, xs)
    return jnp.moveaxis(o, 0, 2).reshape(bsz, nh, L, dv), s_last


def gdn_mixer(h, p, s0_fwd, s0_bwd):
    bsz, L, _ = h.shape
    nh, dh = GDN_HEADS, GDN_HEAD_DIM
    proj = h @ p['gdn_w_in']
    qkv = jax.nn.silu(dwconv1d(proj[..., :3 * D_MODEL], p['gdn_w_conv']))
    z_gate = proj[..., 3 * D_MODEL:4 * D_MODEL].reshape(bsz, L, nh, dh).astype(jnp.float32)
    ab = proj[..., 4 * D_MODEL:].astype(jnp.float32).reshape(bsz, L, 2, 2, nh)
    g = -jnp.exp(p['gdn_a_log'].astype(jnp.float32)) * jax.nn.softplus(ab[:, :, 0] + p['gdn_dt_bias'])
    beta = jax.nn.sigmoid(ab[:, :, 1])
    g = g.transpose(2, 0, 3, 1)
    beta = beta.transpose(2, 0, 3, 1)

    def heads(t):
        return t.reshape(bsz, L, nh, dh).transpose(0, 2, 1, 3).astype(jnp.float32)

    q, k, v = (heads(t) for t in jnp.split(qkv, 3, axis=-1))
    q = l2norm(q) * dh ** -0.5
    k = l2norm(k)
    o_f, s_f = gdn_chunked(q, k, v, g[0], beta[0], s0_fwd)

    def rev(t):
        return jnp.flip(t, axis=2)

    o_b, s_b = gdn_chunked(rev(q), rev(k), rev(v), rev(g[1]), rev(beta[1]), s0_bwd)
    o = (o_f + rev(o_b)).transpose(0, 2, 1, 3)
    o = o * lax.rsqrt(jnp.mean(o * o, axis=-1, keepdims=True) + RMS_EPS) * p['gdn_norm_w'] * jax.nn.silu(z_gate)
    return o.reshape(bsz, L, D_MODEL).astype(h.dtype) @ p['gdn_w_out'], s_f, s_b


def conv_glu(h, p, rows, cols):
    gate, val = jnp.split(h @ p['ffn_w_up'], 2, axis=-1)
    gate = dwconv2d_grid(gate, p['ffn_w_dw'], p['ffn_b_dw'], rows, cols)
    return (jax.nn.gelu(gate, approximate=False) * val) @ p['ffn_w_down']


def setup_inputs(seed: int = 0) -> dict:
    key = jax.random.key(seed)
    ks = iter(jax.random.split(key, 64))
    D, F, H, FH = D_MODEL, D_FF, GDN_HEADS, HY_FILTER_HIDDEN

    def nrm(shape, std):
        return jax.random.normal(next(ks), shape, jnp.float32) * std

    inp = {}
    inp['x'] = nrm((BATCH, SEQ, D), 1.0)
    inp['c'] = nrm((BATCH, D), 1.0)
    inp['ctx'] = nrm((BATCH, CTX_LEN, D), 1.0)
    inp['c_ctx'] = nrm((D,), 1.0)
    for i in range(DEPTH):
        pre = 'l%d_' % i
        inp[pre + 'w_ada'] = nrm((D, 6 * D), 0.5 * D ** -0.5)
        inp[pre + 'b_ada'] = nrm((6 * D,), 0.01)
        inp[pre + 'ln1_g'] = 1.0 + nrm((D,), 0.02)
        inp[pre + 'ln1_b'] = nrm((D,), 0.02)
        inp[pre + 'ln2_g'] = 1.0 + nrm((D,), 0.02)
        inp[pre + 'ln2_b'] = nrm((D,), 0.02)
        if i % N_MIXERS == 0:
            inp[pre + 'hy_w_in'] = nrm((D, (HY_ORDER + 1) * D), D ** -0.5)
            inp[pre + 'hy_b_in'] = nrm(((HY_ORDER + 1) * D,), 0.02)
            inp[pre + 'hy_w_short'] = nrm((HY_SHORT_K, (HY_ORDER + 1) * D), HY_SHORT_K ** -0.5)
            inp[pre + 'hy_b_short'] = nrm(((HY_ORDER + 1) * D,), 0.02)
            inp[pre + 'hy_f_w1'] = nrm((HY_EMB_DIM, FH), HY_EMB_DIM ** -0.5)
            inp[pre + 'hy_f_b1'] = nrm((FH,), 0.02)
            inp[pre + 'hy_f_w2'] = nrm((FH, FH), FH ** -0.5)
            inp[pre + 'hy_f_b2'] = nrm((FH,), 0.02)
            inp[pre + 'hy_f_w3'] = nrm((FH, FH), FH ** -0.5)
            inp[pre + 'hy_f_b3'] = nrm((FH,), 0.02)
            inp[pre + 'hy_f_wout'] = nrm((FH, 2 * HY_ORDER * D), 0.02)
            inp[pre + 'hy_f_freq'] = 1.0 + nrm((FH,), 0.02)
            inp[pre + 'hy_skip'] = nrm((HY_ORDER, D), 1.0)
            inp[pre + 'hy_w_out'] = nrm((D, D), DN_BETA * D ** -0.5)
            inp[pre + 'hy_b_out'] = nrm((D,), 0.02)
        else:
            inp[pre + 'gdn_w_in'] = nrm((D, 4 * D + 4 * H), D ** -0.5)
            inp[pre + 'gdn_w_conv'] = nrm((GDN_CONV_K, 3 * D), GDN_CONV_K ** -0.5)
            inp[pre + 'gdn_a_log'] = jnp.log(jax.random.uniform(next(ks), (2, H), jnp.float32, 1.0, 16.0))
            dt = jnp.exp(jax.random.uniform(next(ks), (2, H), jnp.float32, math.log(1e-3), math.log(1e-1)))
            inp[pre + 'gdn_dt_bias'] = dt + jnp.log(-jnp.expm1(-dt))
            inp[pre + 'gdn_norm_w'] = 1.0 + nrm((GDN_HEAD_DIM,), 0.02)
            inp[pre + 'gdn_w_out'] = nrm((D, D), DN_BETA * D ** -0.5)
        inp[pre + 'ffn_w_up'] = nrm((D, 2 * F), D ** -0.5)
        inp[pre + 'ffn_w_dw'] = nrm((FFN_CONV_K, FFN_CONV_K, F), 1.0 / FFN_CONV_K)
        inp[pre + 'ffn_b_dw'] = nrm((F,), 0.02)
        inp[pre + 'ffn_w_down'] = nrm((F, D), DN_BETA * F ** -0.5)
    return inp


def reference(x, c, ctx, c_ctx,
              l0_w_ada, l0_b_ada, l0_ln1_g, l0_ln1_b, l0_ln2_g, l0_ln2_b,
              l0_hy_w_in, l0_hy_b_in, l0_hy_w_short, l0_hy_b_short,
              l0_hy_f_w1, l0_hy_f_b1, l0_hy_f_w2, l0_hy_f_b2, l0_hy_f_w3, l0_hy_f_b3,
              l0_hy_f_wout, l0_hy_f_freq, l0_hy_skip, l0_hy_w_out, l0_hy_b_out,
              l0_ffn_w_up, l0_ffn_w_dw, l0_ffn_b_dw, l0_ffn_w_down,
              l1_w_ada, l1_b_ada, l1_ln1_g, l1_ln1_b, l1_ln2_g, l1_ln2_b,
              l1_gdn_w_in, l1_gdn_w_conv, l1_gdn_a_log, l1_gdn_dt_bias, l1_gdn_norm_w, l1_gdn_w_out,
              l1_ffn_w_up, l1_ffn_w_dw, l1_ffn_b_dw, l1_ffn_w_down):
    layers = (
        dict(w_ada=l0_w_ada, b_ada=l0_b_ada, ln1_g=l0_ln1_g, ln1_b=l0_ln1_b, ln2_g=l0_ln2_g, ln2_b=l0_ln2_b,
             hy_w_in=l0_hy_w_in, hy_b_in=l0_hy_b_in, hy_w_short=l0_hy_w_short, hy_b_short=l0_hy_b_short,
             hy_f_w1=l0_hy_f_w1, hy_f_b1=l0_hy_f_b1, hy_f_w2=l0_hy_f_w2, hy_f_b2=l0_hy_f_b2,
             hy_f_w3=l0_hy_f_w3, hy_f_b3=l0_hy_f_b3, hy_f_wout=l0_hy_f_wout, hy_f_freq=l0_hy_f_freq,
             hy_skip=l0_hy_skip, hy_w_out=l0_hy_w_out, hy_b_out=l0_hy_b_out,
             ffn_w_up=l0_ffn_w_up, ffn_w_dw=l0_ffn_w_dw, ffn_b_dw=l0_ffn_b_dw, ffn_w_down=l0_ffn_w_down),
        dict(w_ada=l1_w_ada, b_ada=l1_b_ada, ln1_g=l1_ln1_g, ln1_b=l1_ln1_b, ln2_g=l1_ln2_g, ln2_b=l1_ln2_b,
             gdn_w_in=l1_gdn_w_in, gdn_w_conv=l1_gdn_w_conv, gdn_a_log=l1_gdn_a_log,
             gdn_dt_bias=l1_gdn_dt_bias, gdn_norm_w=l1_gdn_norm_w, gdn_w_out=l1_gdn_w_out,
             ffn_w_up=l1_ffn_w_up, ffn_w_dw=l1_ffn_w_dw, ffn_b_dw=l1_ffn_b_dw, ffn_w_down=l1_ffn_w_down),
    )
    rows = x.shape[1] // GRID_W
    ctx_len = ctx.shape[1]
    for i in range(DEPTH):
        p = layers[i]
        last = i == DEPTH - 1
        sh1, sc1, gt1, sh2, sc2, gt2 = adaln(c, p['w_ada'], p['b_ada'])
        csh1, csc1, cgt1, csh2, csc2, cgt2 = adaln(c_ctx, p['w_ada'], p['b_ada'])
        h_lat = modulate(x, sh1, sc1)
        h_ctx = modulate(ctx, csh1, csc1)
        if i % N_MIXERS == 0:
            y_lat = hyena_mixer(h_lat, p)
            y_ctx = None if last else hyena_mixer(h_ctx, p)
        else:
            s0 = jnp.zeros((ctx.shape[0], GDN_HEADS, GDN_HEAD_DIM, GDN_HEAD_DIM), jnp.float32)
            y_ctx, s_fwd, s_bwd = gdn_mixer(h_ctx, p, s0, s0)
            y_lat, _, _ = gdn_mixer(h_lat, p, s_fwd, s_bwd)
        x = post_norm(x, y_lat, gt1, p['ln1_g'], p['ln1_b'])
        x = post_norm(x, conv_glu(modulate(x, sh2, sc2), p, rows, GRID_W), gt2, p['ln2_g'], p['ln2_b'])
        if not last:
            ctx = post_norm(ctx, y_ctx, cgt1, p['ln1_g'], p['ln1_b'])
            ctx = post_norm(ctx, conv_glu(modulate(ctx, csh2, csc2), p, 1, ctx_len), cgt2,
                            p['ln2_g'], p['ln2_b'])
    return x
```

```python
import functools
import math

import numpy as np
import jax
import jax.numpy as jnp
from jax import lax
from jax.experimental import pallas as pl
from jax.experimental.pallas import tpu as pltpu

F32 = jnp.float32
BF16 = jnp.bfloat16

GRID_W = 64
HY_ORDER = 2
HY_EMB_DIM = 33
HY_DECAY_TARGET = 1e-2
HY_FAST_DECAY_PCT = 0.3
HY_SLOW_DECAY_PCT = 1.5
GDN_CHUNK = 64
LN_EPS = 1e-5
RMS_EPS = 1e-6
L2_EPS = 1e-6
DEPTH = 2
DN_ALPHA = (2 * DEPTH) ** 0.25

LANES = 128
SLAB = 128
SLAB_PITCH = SLAB + 8
VMEM_LIMIT = 56 * 1024 * 1024


def _cparams(sem):
    return pltpu.CompilerParams(dimension_semantics=sem, vmem_limit_bytes=VMEM_LIMIT)


def _mm_kernel(a_ref, w_ref, b_ref, o_ref, acc_ref, *, nk):
    part = jnp.dot(a_ref[...].astype(BF16), w_ref[...].astype(BF16), preferred_element_type=F32)
    if nk == 1:
        o_ref[...] = (part + b_ref[...]).astype(o_ref.dtype)
        return
    k = pl.program_id(2)

    @pl.when(k == 0)
    def _():
        acc_ref[...] = part

    @pl.when(k > 0)
    def _():
        acc_ref[...] += part

    @pl.when(k == nk - 1)
    def _():
        o_ref[...] = (acc_ref[...] + b_ref[...]).astype(o_ref.dtype)


def _pick(n, pref):
    if n <= pref:
        return n
    t = (pref // LANES) * LANES
    while t >= LANES:
        if n % t == 0:
            return t
        t -= LANES
    return n


def matmul(a, w, bias=None, out_dtype=F32, tm=1024, tn=1024, tk=4096):
    m, kd = a.shape
    kd2, n = w.shape
    assert kd == kd2
    tm, tn, tk = _pick(m, tm), _pick(n, tn), _pick(kd, tk)
    nk = kd // tk
    if bias is None:
        bias = jnp.zeros((n,), F32)
    bias = bias.reshape(1, n).astype(F32)
    return pl.pallas_call(
        functools.partial(_mm_kernel, nk=nk),
        grid=(m // tm, n // tn, nk),
        in_specs=[pl.BlockSpec((tm, tk), lambda i, j, k: (i, k)),
                  pl.BlockSpec((tk, tn), lambda i, j, k: (k, j)),
                  pl.BlockSpec((1, tn), lambda i, j, k: (0, j))],
        out_specs=pl.BlockSpec((tm, tn), lambda i, j, k: (i, j)),
        out_shape=jax.ShapeDtypeStruct((m, n), out_dtype),
        scratch_shapes=[pltpu.VMEM((tm, tn) if nk > 1 else (8, LANES), F32)],
        compiler_params=_cparams(("parallel", "parallel", "arbitrary")),
        name="matmul",
    )(a, w, bias)


def _real_form(mat):
    return np.block([[mat.real, -mat.imag], [mat.imag, mat.real]])


@functools.lru_cache(maxsize=None)
def _two_stage_consts(n1, n2):
    n = n1 * n2
    h1 = n1 // 2
    a = np.arange(n1)
    b = np.arange(n2)
    ma = np.exp(-2j * np.pi * np.outer(a, a) / n1)
    mb = np.exp(-2j * np.pi * np.outer(b, b) / n2)
    fa_half = _real_form(ma[:, :h1])
    fa_real = np.concatenate([ma.real, ma.imag], axis=0)
    fb = _real_form(mb)
    fbi = _real_form(np.conj(mb))
    fai = _real_form(np.conj(ma)[:h1, :] / n)
    tw = np.exp(-2j * np.pi * b / n)
    tw = np.stack([np.broadcast_to(tw.real[:, None], (n2, LANES)),
                   np.broadcast_to(tw.imag[:, None], (n2, LANES))])
    return dict(fa_half=fa_half, fa_real=fa_real, fb=fb, fbi=fbi, fai=fai, tw=tw)


@functools.lru_cache(maxsize=None)
def _dense_consts(length):
    n = 2 * length
    f = np.arange(n)
    t = np.arange(length)
    fwd = np.exp(-2j * np.pi * np.outer(f, t) / n)
    fwd_full = np.exp(-2j * np.pi * np.outer(f, f) / n)
    inv = np.exp(2j * np.pi * np.outer(t, f) / n) / n
    return dict(fwd=_real_form(fwd),
                fwd_real=np.concatenate([fwd_full.real, fwd_full.imag], axis=0),
                inv=_real_form(inv))


def _cmul(ar, ai, br, bi):
    return ar * br - ai * bi, ar * bi + ai * br


def _twiddle_init(t_ref):
    t_ref[0] = jnp.ones(t_ref.shape[1:], F32)
    t_ref[1] = jnp.zeros(t_ref.shape[1:], F32)


def _twiddle_step(t_ref, tw_ref, tr, ti):
    nr, ni = _cmul(tr, ti, tw_ref[0], tw_ref[1])
    t_ref[0] = nr
    t_ref[1] = ni


def _filter_spectrum_kernel(k_ref, fa_ref, fb_ref, tw_ref, o_ref, bufr, bufi, t_ref, *, n1, sb):
    j = pl.program_id(1)
    pitch = SLAB_PITCH

    @pl.when(j == 0)
    def _():
        def load(a, c):
            src = pl.multiple_of(a * SLAB, SLAB)
            dst = pl.multiple_of(a * pitch, 8)
            bufr[pl.ds(dst, SLAB), :] = k_ref[pl.ds(src, SLAB), :]
            return c
        lax.fori_loop(0, n1, load, 0)
        fa = fa_ref[...]

        def stage_a(b, c):
            x = bufr[pl.ds(b, n1, stride=pitch), :].astype(BF16)
            y = jnp.dot(fa, x, preferred_element_type=F32)
            bufr[pl.ds(b, n1, stride=pitch), :] = y[:n1]
            bufi[pl.ds(b, n1, stride=pitch), :] = y[n1:]
            return c
        lax.fori_loop(0, SLAB, stage_a, 0)
        _twiddle_init(t_ref)

    fb = fb_ref[...]

    def stage_b(s, c):
        k1 = j * sb + s
        row = pl.multiple_of(k1 * pitch, 8)
        tr, ti = t_ref[0], t_ref[1]
        yr, yi = _cmul(bufr[pl.ds(row, SLAB), :], bufi[pl.ds(row, SLAB), :], tr, ti)
        x = jnp.concatenate([yr, yi], axis=0).astype(BF16)
        spec = jnp.dot(fb, x, preferred_element_type=F32)
        out = pl.multiple_of(s * SLAB, SLAB)
        o_ref[0, pl.ds(out, SLAB), :] = spec[:SLAB]
        o_ref[1, pl.ds(out, SLAB), :] = spec[SLAB:]
        _twiddle_step(t_ref, tw_ref, tr, ti)
        return c
    lax.fori_loop(0, sb, stage_b, 0)


def _slab_blocks(n1):
    return 8 if n1 % 8 == 0 and n1 >= 16 else 1


def filter_spectrum(k_time):
    n, d = k_time.shape
    n1 = n // SLAB
    kb = _slab_blocks(n1)
    sb = n1 // kb
    c = _two_stage_consts(n1, SLAB)
    fa = jnp.asarray(c["fa_real"], BF16)
    fb = jnp.asarray(c["fb"], BF16)
    tw = jnp.asarray(c["tw"], F32)
    const = lambda shape: pl.BlockSpec(shape, lambda i, j: (0,) * len(shape))
    return pl.pallas_call(
        functools.partial(_filter_spectrum_kernel, n1=n1, sb=sb),
        grid=(d // LANES, kb),
        in_specs=[pl.BlockSpec((n, LANES), lambda i, j: (0, i)),
                  const(fa.shape), const(fb.shape), const(tw.shape)],
        out_specs=pl.BlockSpec((2, sb * SLAB, LANES), lambda i, j: (0, j, i)),
        out_shape=jax.ShapeDtypeStruct((2, n, d), F32),
        scratch_shapes=[pltpu.VMEM((n1 * SLAB_PITCH, LANES), F32),
                        pltpu.VMEM((n1 * SLAB_PITCH, LANES), F32),
                        pltpu.VMEM((2, SLAB, LANES), F32)],
        compiler_params=_cparams(("parallel", "arbitrary")),
        name="filter_spectrum",
    )(k_time, fa, fb, tw)


def _fftconv_kernel(z_ref, g_ref, ks_ref, skip_ref, fa_ref, fb_ref, fbi_ref, fai_ref, tw_ref, o_ref,
                    bufr, bufi, t_ref, *, n1, sb, kb):
    j = pl.program_id(1)
    pitch = SLAB_PITCH
    h1 = n1 // 2

    @pl.when(j == 0)
    def _():
        def load(a, c):
            src = pl.multiple_of(a * SLAB, SLAB)
            dst = pl.multiple_of(a * pitch, 8)
            bufr[pl.ds(dst, SLAB), :] = z_ref[0, pl.ds(src, SLAB), :].astype(F32)
            bufi[pl.ds(dst, SLAB), :] = z_ref[1, pl.ds(src, SLAB), :].astype(F32)
            return c
        lax.fori_loop(0, h1, load, 0)
        fa = fa_ref[...]

        def stage_a(b, c):
            x = jnp.concatenate([bufr[pl.ds(b, h1, stride=pitch), :],
                                 bufi[pl.ds(b, h1, stride=pitch), :]], axis=0).astype(BF16)
            y = jnp.dot(fa, x, preferred_element_type=F32)
            bufr[pl.ds(b, n1, stride=pitch), :] = y[:n1]
            bufi[pl.ds(b, n1, stride=pitch), :] = y[n1:]
            return c
        lax.fori_loop(0, SLAB, stage_a, 0)
        _twiddle_init(t_ref)

    fb = fb_ref[...]
    fbi = fbi_ref[...]

    def stage_b(s, c):
        k1 = j * sb + s
        row = pl.multiple_of(k1 * pitch, 8)
        tr, ti = t_ref[0], t_ref[1]
        yr, yi = _cmul(bufr[pl.ds(row, SLAB), :], bufi[pl.ds(row, SLAB), :], tr, ti)
        spec = jnp.dot(fb, jnp.concatenate([yr, yi], axis=0).astype(BF16), preferred_element_type=F32)
        krow = pl.multiple_of(s * SLAB, SLAB)
        pr, pi = _cmul(spec[:SLAB], spec[SLAB:], ks_ref[0, pl.ds(krow, SLAB), :], ks_ref[1, pl.ds(krow, SLAB), :])
        back = jnp.dot(fbi, jnp.concatenate([pr, pi], axis=0).astype(BF16), preferred_element_type=F32)
        ur, ui = _cmul(back[:SLAB], back[SLAB:], tr, -ti)
        bufr[pl.ds(row, SLAB), :] = ur
        bufi[pl.ds(row, SLAB), :] = ui
        _twiddle_step(t_ref, tw_ref, tr, ti)
        return c
    lax.fori_loop(0, sb, stage_b, 0)

    @pl.when(j == kb - 1)
    def _():
        fai = fai_ref[...]

        def stage_ai(b, c):
            x = jnp.concatenate([bufr[pl.ds(b, n1, stride=pitch), :],
                                 bufi[pl.ds(b, n1, stride=pitch), :]], axis=0).astype(BF16)
            y = jnp.dot(fai, x, preferred_element_type=F32)
            bufr[pl.ds(b, h1, stride=pitch), :] = y[:h1]
            bufi[pl.ds(b, h1, stride=pitch), :] = y[h1:]
            return c
        lax.fori_loop(0, SLAB, stage_ai, 0)
        skip = skip_ref[...]

        def store(a, c):
            dst = pl.multiple_of(a * SLAB, SLAB)
            src = pl.multiple_of(a * pitch, 8)
            for bi, buf in enumerate((bufr, bufi)):
                zz = z_ref[bi, pl.ds(dst, SLAB), :].astype(F32)
                gg = g_ref[bi, pl.ds(dst, SLAB), :].astype(F32)
                o_ref[bi, pl.ds(dst, SLAB), :] = (gg * (buf[pl.ds(src, SLAB), :] + zz * skip)).astype(o_ref.dtype)
            return c
        lax.fori_loop(0, h1, store, 0)


def fftconv_gated(z, gate, kspec, skip, out_dtype=BF16):
    bsz, length, d = z.shape
    assert bsz == 2
    n1 = 2 * length // SLAB
    kb = _slab_blocks(n1)
    sb = n1 // kb
    c = _two_stage_consts(n1, SLAB)
    fa = jnp.asarray(c["fa_half"], BF16)
    fb = jnp.asarray(c["fb"], BF16)
    fbi = jnp.asarray(c["fbi"], BF16)
    fai = jnp.asarray(c["fai"], BF16)
    tw = jnp.asarray(c["tw"], F32)
    const = lambda shape: pl.BlockSpec(shape, lambda i, j: (0,) * len(shape))
    seq = pl.BlockSpec((2, length, LANES), lambda i, j: (0, 0, i))
    return pl.pallas_call(
        functools.partial(_fftconv_kernel, n1=n1, sb=sb, kb=kb),
        grid=(d // LANES, kb),
        in_specs=[seq, seq,
                  pl.BlockSpec((2, sb * SLAB, LANES), lambda i, j: (0, j, i)),
                  pl.BlockSpec((1, LANES), lambda i, j: (0, i)),
                  const(fa.shape), const(fb.shape), const(fbi.shape), const(fai.shape), const(tw.shape)],
        out_specs=seq,
        out_shape=jax.ShapeDtypeStruct((2, length, d), out_dtype),
        scratch_shapes=[pltpu.VMEM((n1 * SLAB_PITCH, LANES), F32),
                        pltpu.VMEM((n1 * SLAB_PITCH, LANES), F32),
                        pltpu.VMEM((2, SLAB, LANES), F32)],
        compiler_params=_cparams(("parallel", "arbitrary")),
        name="fftconv",
    )(z, gate, kspec, skip.reshape(1, d), fa, fb, fbi, fai, tw)


def _dense_conv_kernel(z_ref, g_ref, k_ref, skip_ref, fwd_ref, fwdk_ref, inv_ref, o_ref, *, length):
    n = 2 * length
    x = jnp.concatenate([z_ref[0], z_ref[1]], axis=0)
    spec = jnp.dot(fwd_ref[...], x.astype(BF16), preferred_element_type=F32)
    kspec = jnp.dot(fwdk_ref[...], k_ref[...].astype(BF16), preferred_element_type=F32)
    pr, pi = _cmul(spec[:n], spec[n:], kspec[:n], kspec[n:])
    y = jnp.dot(inv_ref[...], jnp.concatenate([pr, pi], axis=0).astype(BF16), preferred_element_type=F32)
    skip = skip_ref[...]
    for bi in range(2):
        yy = y[bi * length:(bi + 1) * length]
        o_ref[bi] = (g_ref[bi].astype(F32) * (yy + z_ref[bi].astype(F32) * skip)).astype(o_ref.dtype)


def dense_conv_gated(z, gate, k_time, skip, out_dtype=BF16):
    bsz, length, d = z.shape
    assert bsz == 2
    c = _dense_consts(length)
    fwd = jnp.asarray(c["fwd"], BF16)
    fwdk = jnp.asarray(c["fwd_real"], BF16)
    inv = jnp.asarray(c["inv"], BF16)
    tile = 2 * LANES if d % (2 * LANES) == 0 else LANES
    const = lambda shape: pl.BlockSpec(shape, lambda i: (0,) * len(shape))
    seq = pl.BlockSpec((2, length, tile), lambda i: (0, 0, i))
    return pl.pallas_call(
        functools.partial(_dense_conv_kernel, length=length),
        grid=(d // tile,),
        in_specs=[seq, seq, pl.BlockSpec((2 * length, tile), lambda i: (0, i)),
                  pl.BlockSpec((1, tile), lambda i: (0, i)),
                  const(fwd.shape), const(fwdk.shape), const(inv.shape)],
        out_specs=seq,
        out_shape=jax.ShapeDtypeStruct((2, length, d), out_dtype),
        compiler_params=_cparams(("parallel",)),
        name="dense_conv",
    )(z, gate, k_time, skip.reshape(1, d), fwd, fwdk, inv)


def _dot_nt(a, b):
    return lax.dot_general(a.astype(BF16), b.astype(BF16), (((1,), (1,)), ((), ())), preferred_element_type=F32)


def _dot_tn(a, b):
    return lax.dot_general(a.astype(BF16), b.astype(BF16), (((0,), (0,)), ((), ())), preferred_element_type=F32)


def _dot(a, b):
    return jnp.dot(a.astype(BF16), b.astype(BF16), preferred_element_type=F32)


def _dot3(a, b):
    ah = a.astype(BF16)
    bh = b.astype(BF16)
    al = (a - ah.astype(F32)).astype(BF16)
    bl = (b - bh.astype(F32)).astype(BF16)
    dot = functools.partial(jnp.dot, preferred_element_type=F32)
    return dot(ah, bh) + (dot(ah, bl) + dot(al, bh))


def _gdn_kernel(q_ref, k_ref, v_ref, gcol_ref, bcol_ref, grow_ref, s0_ref, o_ref, sout_ref,
                s_ref, u_ref, w_ref, qg_ref, kg_ref, at_ref, gl_ref, *, r, nb, nh):
    cs = GDN_CHUNK
    head = pl.program_id(1)
    rev = pl.program_id(2) == 1
    j = pl.program_id(3)

    @pl.when(j == 0)
    def _():
        s_ref[...] = s0_ref[0, 0, 0]

    ii = lax.broadcasted_iota(jnp.int32, (cs, cs), 0)
    jj = lax.broadcasted_iota(jnp.int32, (cs, cs), 1)
    after = jnp.where(rev, jj - ii, ii - jj)
    incl = after >= 0
    strict = after > 0
    eye = (ii == jj).astype(F32)
    blk_same = [jnp.right_shift(ii, lvl) == jnp.right_shift(jj, lvl) for lvl in range(int(math.log2(cs)) + 1)]
    head_lane = lax.broadcasted_iota(jnp.int32, (cs, nh), 1) == head
    last_lane = lax.broadcasted_iota(jnp.int32, (1, cs), 1) == jnp.where(rev, 0, cs - 1)

    for c in range(r):
        rows = slice(c * cs, (c + 1) * cs)
        q = q_ref[0, rows, :]
        k = k_ref[0, rows, :]
        v = v_ref[0, rows, :]
        gc = jnp.sum(jnp.where(head_lane, gcol_ref[0, 0, rows, :], 0.0), axis=1, keepdims=True)
        bc = jnp.sum(jnp.where(head_lane, bcol_ref[0, 0, rows, :], 0.0), axis=1, keepdims=True)
        gr = grow_ref[0, 0, 0, c:c + 1, :]
        g_last = jnp.sum(jnp.where(last_lane, gr, 0.0), axis=1, keepdims=True)
        decay = jnp.where(incl, jnp.exp(jnp.where(incl, gc - gr, 0.0)), 0.0)
        kb = k * bc
        a_mat = jnp.where(strict, _dot_nt(kb, k) * decay, 0.0)
        t = eye - jnp.where(blk_same[1] & ~blk_same[0], a_mat, 0.0)
        for lvl in range(1, len(blk_same) - 1):
            x = jnp.where(blk_same[lvl + 1] & ~blk_same[lvl], a_mat, 0.0)
            t = t - _dot3(t, _dot3(x, t))
        eg = jnp.exp(gc)
        sol = _dot(t, jnp.concatenate([v * bc, kb * eg], axis=1))
        u_ref[c] = sol[:, :LANES]
        w_ref[c] = sol[:, LANES:]
        at_ref[c] = jnp.where(incl, _dot_nt(q, k) * decay, 0.0)
        qg_ref[c] = q * eg
        kg_ref[c] = k * jnp.exp(g_last - gc)
        gl_ref[c] = jnp.broadcast_to(jnp.exp(g_last), (8, LANES))

    def scan(p, carry):
        c = jnp.where(rev, r - 1 - p, p)
        s = s_ref[...]
        v_new = u_ref[c] - _dot(w_ref[c], s)
        o = _dot(qg_ref[c], s) + _dot(at_ref[c], v_new)
        s_ref[...] = s * gl_ref[c][0:1, :] + _dot_tn(kg_ref[c], v_new)
        o_ref[0, 0, pl.ds(pl.multiple_of(c * cs, cs), cs), :] = o
        return carry
    lax.fori_loop(0, r, scan, 0)

    @pl.when(j == nb - 1)
    def _():
        sout_ref[0, 0, 0] = s_ref[...]


def gdn_scan(qkv, gcol, bcol, grow, s0, nh):
    bsz, length, d3 = qkv.shape
    d = d3 // 3
    dh = d // nh
    assert dh == LANES
    nch = length // GDN_CHUNK
    r = 8 if nch % 8 == 0 else nch
    nb = nch // r
    rows = r * GDN_CHUNK
    blk = lambda j, dr: j + dr * (nb - 1 - 2 * j)
    seq = lambda off: pl.BlockSpec((1, rows, LANES), lambda b, h, dr, j: (b, blk(j, dr), off * nh + h))
    col = pl.BlockSpec((1, 1, rows, nh), lambda b, h, dr, j: (b, dr, blk(j, dr), 0))
    state = pl.BlockSpec((1, 1, 1, dh, dh), lambda b, h, dr, j: (b, dr, h, 0, 0))
    per_chunk = lambda *shape: pltpu.VMEM((r,) + shape, F32)
    return pl.pallas_call(
        functools.partial(_gdn_kernel, r=r, nb=nb, nh=nh),
        grid=(bsz, nh, 2, nb),
        in_specs=[seq(0), seq(1), seq(2), col, col,
                  pl.BlockSpec((1, 1, 1, r, GDN_CHUNK), lambda b, h, dr, j: (b, dr, h, blk(j, dr), 0)),
                  state],
        out_specs=[pl.BlockSpec((1, 1, rows, LANES), lambda b, h, dr, j: (dr, b, blk(j, dr), h)), state],
        out_shape=[jax.ShapeDtypeStruct((2, bsz, length, d), F32),
                   jax.ShapeDtypeStruct((bsz, 2, nh, dh, dh), F32)],
        scratch_shapes=[pltpu.VMEM((dh, dh), F32),
                        per_chunk(GDN_CHUNK, LANES), per_chunk(GDN_CHUNK, LANES),
                        per_chunk(GDN_CHUNK, LANES), per_chunk(GDN_CHUNK, LANES),
                        per_chunk(GDN_CHUNK, GDN_CHUNK), per_chunk(8, LANES)],
        compiler_params=_cparams(("parallel", "parallel", "parallel", "arbitrary")),
        name="gdn_scan",
    )(qkv, qkv, qkv, gcol, bcol, grow, s0)


def _layer_norm(x, g, b):
    mu = jnp.mean(x, axis=-1, keepdims=True)
    var = jnp.mean(jnp.square(x - mu), axis=-1, keepdims=True)
    return (x - mu) * lax.rsqrt(var + LN_EPS) * g + b


def _post_norm(x, y, gate, g, b):
    return _layer_norm(DN_ALPHA * x + gate * y, g, b)


def _modulate(x, shift, scale):
    return x * (1 + scale) + shift


def _dwconv1d(x, w):
    zero = jnp.zeros_like(x[:, :1])
    prev = jnp.concatenate([zero, x[:, :-1]], axis=1)
    nxt = jnp.concatenate([x[:, 1:], zero], axis=1)
    return prev * w[0] + x * w[1] + nxt * w[2]


def _dwconv2d_grid(x, w, b, rows, cols):
    bsz, n_tok, ch = x.shape
    img = jnp.pad(x.reshape(bsz, rows, cols, ch), ((0, 0), (1, 1), (1, 1), (0, 0)))
    y = sum(img[:, di:di + rows, dj:dj + cols] * w[di, dj] for di in range(3) for dj in range(3))
    return y.reshape(bsz, n_tok, ch) + b


def _proj(h, w, bias=None, out_dtype=F32, **tiles):
    bsz, length, kd = h.shape
    return matmul(h.reshape(bsz * length, kd).astype(BF16), w, bias, out_dtype, **tiles).reshape(bsz, length, -1)


def _hyena_filter_times(length, p, d):
    t = jnp.linspace(0.0, 1.0, length, dtype=F32)[:, None]
    bands = (HY_EMB_DIM - 1) // 2
    ang = 2.0 * math.pi * jnp.arange(length, dtype=F32)[:, None] / length
    f = jnp.linspace(1e-4, bands - 1, bands, dtype=F32)[None, :]
    z = jnp.concatenate([t, jnp.cos(f * ang), -jnp.sin(f * ang)], axis=-1)
    freq = p['hy_f_freq']
    h = jnp.sin(freq * (z @ p['hy_f_w1'] + p['hy_f_b1']))
    h = jnp.sin(freq * (h @ p['hy_f_w2'] + p['hy_f_b2']))
    h = jnp.sin(freq * (h @ p['hy_f_w3'] + p['hy_f_b3']))
    h = matmul(h, p['hy_f_wout']).reshape(length, 2 * HY_ORDER, d)
    max_decay = math.log(HY_DECAY_TARGET) / HY_FAST_DECAY_PCT
    min_decay = math.log(HY_DECAY_TARGET) / HY_SLOW_DECAY_PCT
    deltas = jnp.abs(jnp.linspace(min_decay, max_decay, d, dtype=F32))
    filt = h * jnp.exp(-t[:, :, None] * deltas)
    out = []
    for o in range(HY_ORDER):
        hf, hb = filt[:, 2 * o], filt[:, 2 * o + 1]
        out.append(jnp.concatenate([hf[:1] + hb[:1], hf[1:], jnp.zeros_like(hf[:1]), hb[:0:-1]], axis=0))
    return out


def _hyena_mixer(h, p, w_in, w_out):
    bsz, length, d = h.shape
    u = (_dwconv1d(_proj(h, w_in, p['hy_b_in']), p['hy_w_short']) + p['hy_b_short']).astype(BF16)
    parts = jnp.split(u, HY_ORDER + 1, axis=-1)
    k_times = _hyena_filter_times(length, p, d)
    z = parts[-1]
    two_stage = (2 * length) % (16 * SLAB) == 0
    for o in range(HY_ORDER):
        if two_stage:
            z = fftconv_gated(z, parts[o], filter_spectrum(k_times[o]), p['hy_skip'][o])
        else:
            z = dense_conv_gated(z, parts[o], k_times[o], p['hy_skip'][o])
    return _proj(z, w_out, p['hy_b_out'])


def _gdn_mixer(h, p, w_qkvz, w_ab, w_out, s0):
    bsz, length, d = h.shape
    nh = p['gdn_a_log'].shape[1]
    dh = d // nh
    proj = _proj(h, w_qkvz)
    ab = _proj(h, w_ab).reshape(bsz, length, 2, 2, nh)
    qkv = jax.nn.silu(_dwconv1d(proj[..., :3 * d], p['gdn_w_conv']))
    z_gate = proj[..., 3 * d:].reshape(bsz, length, nh, dh)

    def l2n(t, scale):
        t = t.reshape(bsz, length, nh, dh)
        t = t * lax.rsqrt(jnp.sum(t * t, axis=-1, keepdims=True) + L2_EPS) * scale
        return t.reshape(bsz, length, d)

    q, k, v = jnp.split(qkv, 3, axis=-1)
    qkv = jnp.concatenate([l2n(q, dh ** -0.5), l2n(k, 1.0), v], axis=-1)
    g = -jnp.exp(p['gdn_a_log']) * jax.nn.softplus(ab[:, :, 0] + p['gdn_dt_bias'])
    beta = jax.nn.sigmoid(ab[:, :, 1]).transpose(0, 2, 1, 3)
    nch = length // GDN_CHUNK
    g = g.transpose(0, 2, 1, 3).reshape(bsz, 2, nch, GDN_CHUNK, nh)
    g = jnp.stack([jnp.cumsum(g[:, 0], axis=2),
                   jnp.flip(jnp.cumsum(jnp.flip(g[:, 1], axis=2), axis=2), axis=2)], axis=1)
    gcol = g.reshape(bsz, 2, length, nh)
    grow = g.transpose(0, 1, 4, 2, 3)
    o, s_out = gdn_scan(qkv, gcol, beta, grow, s0, nh)
    o = (o[0] + o[1]).reshape(bsz, length, nh, dh)
    o = o * lax.rsqrt(jnp.mean(o * o, axis=-1, keepdims=True) + RMS_EPS) * p['gdn_norm_w'] * jax.nn.silu(z_gate)
    return _proj(o.reshape(bsz, length, d), w_out), s_out


def _conv_glu(h, p, w_up, w_down, rows, cols):
    f = p['ffn_b_dw'].shape[0]
    up = _proj(h, w_up)
    gate = _dwconv2d_grid(up[..., :f], p['ffn_w_dw'], p['ffn_b_dw'], rows, cols)
    act = jax.nn.gelu(gate, approximate=False) * up[..., f:]
    return _proj(act, w_down)


def kernel(x, c, ctx, c_ctx, l0_w_ada, l0_b_ada, l0_ln1_g, l0_ln1_b, l0_ln2_g, l0_ln2_b, l0_hy_w_in, l0_hy_b_in, l0_hy_w_short, l0_hy_b_short, l0_hy_f_w1, l0_hy_f_b1, l0_hy_f_w2, l0_hy_f_b2, l0_hy_f_w3, l0_hy_f_b3, l0_hy_f_wout, l0_hy_f_freq, l0_hy_skip, l0_hy_w_out, l0_hy_b_out, l0_ffn_w_up, l0_ffn_w_dw, l0_ffn_b_dw, l0_ffn_w_down, l1_w_ada, l1_b_ada, l1_ln1_g, l1_ln1_b, l1_ln2_g, l1_ln2_b, l1_gdn_w_in, l1_gdn_w_conv, l1_gdn_a_log, l1_gdn_dt_bias, l1_gdn_norm_w, l1_gdn_w_out, l1_ffn_w_up, l1_ffn_w_dw, l1_ffn_b_dw, l1_ffn_w_down):
    layers = (
        dict(w_ada=l0_w_ada, b_ada=l0_b_ada, ln1_g=l0_ln1_g, ln1_b=l0_ln1_b, ln2_g=l0_ln2_g, ln2_b=l0_ln2_b,
             hy_w_in=l0_hy_w_in, hy_b_in=l0_hy_b_in, hy_w_short=l0_hy_w_short, hy_b_short=l0_hy_b_short,
             hy_f_w1=l0_hy_f_w1, hy_f_b1=l0_hy_f_b1, hy_f_w2=l0_hy_f_w2, hy_f_b2=l0_hy_f_b2,
             hy_f_w3=l0_hy_f_w3, hy_f_b3=l0_hy_f_b3, hy_f_wout=l0_hy_f_wout, hy_f_freq=l0_hy_f_freq,
             hy_skip=l0_hy_skip, hy_w_out=l0_hy_w_out, hy_b_out=l0_hy_b_out,
             ffn_w_up=l0_ffn_w_up, ffn_w_dw=l0_ffn_w_dw, ffn_b_dw=l0_ffn_b_dw, ffn_w_down=l0_ffn_w_down),
        dict(w_ada=l1_w_ada, b_ada=l1_b_ada, ln1_g=l1_ln1_g, ln1_b=l1_ln1_b, ln2_g=l1_ln2_g, ln2_b=l1_ln2_b,
             gdn_w_in=l1_gdn_w_in, gdn_w_conv=l1_gdn_w_conv, gdn_a_log=l1_gdn_a_log,
             gdn_dt_bias=l1_gdn_dt_bias, gdn_norm_w=l1_gdn_norm_w, gdn_w_out=l1_gdn_w_out,
             ffn_w_up=l1_ffn_w_up, ffn_w_dw=l1_ffn_w_dw, ffn_b_dw=l1_ffn_b_dw, ffn_w_down=l1_ffn_w_down),
    )
    bsz, seq, d = x.shape
    rows = seq // GRID_W
    ctx_len = ctx.shape[1]
    cond = jnp.concatenate([c, c_ctx[None], jnp.zeros((8 - bsz - 1, d), F32)], axis=0)
    for i, p in enumerate(layers):
        last = i == DEPTH - 1
        mod = matmul(jax.nn.silu(cond), p['w_ada'], p['b_ada'], tn=512)
        sh1, sc1, gt1, sh2, sc2, gt2 = jnp.split(mod[:bsz, None, :], 6, axis=-1)
        csh1, csc1, cgt1, csh2, csc2, cgt2 = jnp.split(mod[bsz:bsz + 1], 6, axis=-1)
        h_lat = _modulate(x, sh1, sc1).astype(BF16)
        h_ctx = _modulate(ctx, csh1, csc1).astype(BF16)
        if 'hy_w_in' in p:
            w_in, w_out = p['hy_w_in'].astype(BF16), p['hy_w_out'].astype(BF16)
            y_lat = _hyena_mixer(h_lat, p, w_in, w_out)
            y_ctx = None if last else _hyena_mixer(h_ctx, p, w_in, w_out)
        else:
            nh = p['gdn_a_log'].shape[1]
            w_qkvz = p['gdn_w_in'][:, :4 * d].astype(BF16)
            w_ab = p['gdn_w_in'][:, 4 * d:].astype(BF16)
            w_out = p['gdn_w_out'].astype(BF16)
            s0 = jnp.zeros((bsz, 2, nh, d // nh, d // nh), F32)
            y_ctx, s_ctx = _gdn_mixer(h_ctx, p, w_qkvz, w_ab, w_out, s0)
            y_lat, _ = _gdn_mixer(h_lat, p, w_qkvz, w_ab, w_out, s_ctx)
        w_up, w_down = p['ffn_w_up'].astype(BF16), p['ffn_w_down'].astype(BF16)
        x = _post_norm(x, y_lat, gt1, p['ln1_g'], p['ln1_b'])
        ffn = _conv_glu(_modulate(x, sh2, sc2).astype(BF16), p, w_up, w_down, rows, GRID_W)
        x = _post_norm(x, ffn, gt2, p['ln2_g'], p['ln2_b'])
        if not last:
            ctx = _post_norm(ctx, y_ctx, cgt1, p['ln1_g'], p['ln1_b'])
            ffn = _conv_glu(_modulate(ctx, csh2, csc2).astype(BF16), p, w_up, w_down, 1, ctx_len)
            ctx = _post_norm(ctx, ffn, cgt2, p['ln2_g'], p['ln2_b'])
    return x
```

```python
import functools
import math

import numpy as np
import jax
import jax.numpy as jnp
from jax import lax
from jax.experimental import pallas as pl
from jax.experimental.pallas import tpu as pltpu

F32 = jnp.float32
BF16 = jnp.bfloat16

GRID_W = 64
HY_ORDER = 2
HY_EMB_DIM = 33
HY_DECAY_TARGET = 1e-2
HY_FAST_DECAY_PCT = 0.3
HY_SLOW_DECAY_PCT = 1.5
GDN_CHUNK = 64
LN_EPS = 1e-5
RMS_EPS = 1e-6
L2_EPS = 1e-6
DEPTH = 2
DN_ALPHA = (2 * DEPTH) ** 0.25

LANES = 128
SLAB = 128
SLAB_PITCH = SLAB + 8
VMEM_LIMIT = 56 * 1024 * 1024


def _cparams(sem):
    return pltpu.CompilerParams(dimension_semantics=sem, vmem_limit_bytes=VMEM_LIMIT)


def _mm_kernel(a_ref, w_ref, b_ref, o_ref, acc_ref, *, nk):
    part = jnp.dot(a_ref[...].astype(BF16), w_ref[...].astype(BF16), preferred_element_type=F32)
    if nk == 1:
        o_ref[...] = (part + b_ref[...]).astype(o_ref.dtype)
        return
    k = pl.program_id(2)

    @pl.when(k == 0)
    def _():
        acc_ref[...] = part

    @pl.when(k > 0)
    def _():
        acc_ref[...] += part

    @pl.when(k == nk - 1)
    def _():
        o_ref[...] = (acc_ref[...] + b_ref[...]).astype(o_ref.dtype)


def _pick(n, pref):
    if n <= pref:
        return n
    t = (pref // LANES) * LANES
    while t >= LANES:
        if n % t == 0:
            return t
        t -= LANES
    return n


def matmul(a, w, bias=None, out_dtype=F32, tm=1024, tn=1024, tk=4096):
    m, kd = a.shape
    kd2, n = w.shape
    assert kd == kd2
    tm, tn, tk = _pick(m, tm), _pick(n, tn), _pick(kd, tk)
    nk = kd // tk
    if bias is None:
        bias = jnp.zeros((n,), F32)
    bias = bias.reshape(1, n).astype(F32)
    return pl.pallas_call(
        functools.partial(_mm_kernel, nk=nk),
        grid=(m // tm, n // tn, nk),
        in_specs=[pl.BlockSpec((tm, tk), lambda i, j, k: (i, k)),
                  pl.BlockSpec((tk, tn), lambda i, j, k: (k, j)),
                  pl.BlockSpec((1, tn), lambda i, j, k: (0, j))],
        out_specs=pl.BlockSpec((tm, tn), lambda i, j, k: (i, j)),
        out_shape=jax.ShapeDtypeStruct((m, n), out_dtype),
        scratch_shapes=[pltpu.VMEM((tm, tn) if nk > 1 else (8, LANES), F32)],
        compiler_params=_cparams(("parallel", "parallel", "arbitrary")),
        name="matmul",
    )(a, w, bias)


def _real_form(mat):
    return np.block([[mat.real, -mat.imag], [mat.imag, mat.real]])


@functools.lru_cache(maxsize=None)
def _two_stage_consts(n1, n2):
    n = n1 * n2
    h1 = n1 // 2
    a = np.arange(n1)
    b = np.arange(n2)
    ma = np.exp(-2j * np.pi * np.outer(a, a) / n1)
    mb = np.exp(-2j * np.pi * np.outer(b, b) / n2)
    fa_half = _real_form(ma[:, :h1])
    fa_real = np.concatenate([ma.real, ma.imag], axis=0)
    fb = _real_form(mb)
    fbi = _real_form(np.conj(mb))
    fai = _real_form(np.conj(ma)[:h1, :] / n)
    tw = np.exp(-2j * np.pi * b / n)
    tw = np.stack([np.broadcast_to(tw.real[:, None], (n2, LANES)),
                   np.broadcast_to(tw.imag[:, None], (n2, LANES))])
    return dict(fa_half=fa_half, fa_real=fa_real, fb=fb, fbi=fbi, fai=fai, tw=tw)


@functools.lru_cache(maxsize=None)
def _dense_consts(length):
    n = 2 * length
    f = np.arange(n)
    t = np.arange(length)
    fwd = np.exp(-2j * np.pi * np.outer(f, t) / n)
    fwd_full = np.exp(-2j * np.pi * np.outer(f, f) / n)
    inv = np.exp(2j * np.pi * np.outer(t, f) / n) / n
    return dict(fwd=_real_form(fwd),
                fwd_real=np.concatenate([fwd_full.real, fwd_full.imag], axis=0),
                inv=_real_form(inv))


def _cmul(ar, ai, br, bi):
    return ar * br - ai * bi, ar * bi + ai * br


def _twiddle_init(t_ref):
    t_ref[0] = jnp.ones(t_ref.shape[1:], F32)
    t_ref[1] = jnp.zeros(t_ref.shape[1:], F32)


def _twiddle_step(t_ref, tw_ref, tr, ti):
    nr, ni = _cmul(tr, ti, tw_ref[0], tw_ref[1])
    t_ref[0] = nr
    t_ref[1] = ni


def _filter_spectrum_kernel(k_ref, fa_ref, fb_ref, tw_ref, o_ref, bufr, bufi, t_ref, *, n1, sb):
    j = pl.program_id(1)
    pitch = SLAB_PITCH

    @pl.when(j == 0)
    def _():
        def load(a, c):
            src = pl.multiple_of(a * SLAB, SLAB)
            dst = pl.multiple_of(a * pitch, 8)
            bufr[pl.ds(dst, SLAB), :] = k_ref[pl.ds(src, SLAB), :]
            return c
        lax.fori_loop(0, n1, load, 0)
        fa = fa_ref[...]

        def stage_a(b, c):
            x = bufr[pl.ds(b, n1, stride=pitch), :].astype(BF16)
            y = jnp.dot(fa, x, preferred_element_type=F32)
            bufr[pl.ds(b, n1, stride=pitch), :] = y[:n1]
            bufi[pl.ds(b, n1, stride=pitch), :] = y[n1:]
            return c
        lax.fori_loop(0, SLAB, stage_a, 0, unroll=4)
        _twiddle_init(t_ref)

    fb = fb_ref[...]

    def stage_b(s, c):
        k1 = j * sb + s
        row = pl.multiple_of(k1 * pitch, 8)
        tr, ti = t_ref[0], t_ref[1]
        yr, yi = _cmul(bufr[pl.ds(row, SLAB), :], bufi[pl.ds(row, SLAB), :], tr, ti)
        x = jnp.concatenate([yr, yi], axis=0).astype(BF16)
        spec = jnp.dot(fb, x, preferred_element_type=F32)
        out = pl.multiple_of(s * SLAB, SLAB)
        o_ref[0, pl.ds(out, SLAB), :] = spec[:SLAB]
        o_ref[1, pl.ds(out, SLAB), :] = spec[SLAB:]
        _twiddle_step(t_ref, tw_ref, tr, ti)
        return c
    lax.fori_loop(0, sb, stage_b, 0, unroll=4)


def _slab_blocks(n1):
    return 8 if n1 % 8 == 0 and n1 >= 16 else 1


def filter_spectrum(k_time):
    n, d = k_time.shape
    n1 = n // SLAB
    kb = _slab_blocks(n1)
    sb = n1 // kb
    c = _two_stage_consts(n1, SLAB)
    fa = jnp.asarray(c["fa_real"], BF16)
    fb = jnp.asarray(c["fb"], BF16)
    tw = jnp.asarray(c["tw"], F32)
    const = lambda shape: pl.BlockSpec(shape, lambda i, j: (0,) * len(shape))
    return pl.pallas_call(
        functools.partial(_filter_spectrum_kernel, n1=n1, sb=sb),
        grid=(d // LANES, kb),
        in_specs=[pl.BlockSpec((n, LANES), lambda i, j: (0, i)),
                  const(fa.shape), const(fb.shape), const(tw.shape)],
        out_specs=pl.BlockSpec((2, sb * SLAB, LANES), lambda i, j: (0, j, i)),
        out_shape=jax.ShapeDtypeStruct((2, n, d), F32),
        scratch_shapes=[pltpu.VMEM((n1 * SLAB_PITCH, LANES), F32),
                        pltpu.VMEM((n1 * SLAB_PITCH, LANES), F32),
                        pltpu.VMEM((2, SLAB, LANES), F32)],
        compiler_params=_cparams(("parallel", "arbitrary")),
        name="filter_spectrum",
    )(k_time, fa, fb, tw)


def _fftconv_kernel(z_ref, g_ref, ks_ref, skip_ref, fa_ref, fb_ref, fbi_ref, fai_ref, tw_ref, o_ref,
                    bufr, bufi, t_ref, *, n1, sb, kb):
    j = pl.program_id(1)
    pitch = SLAB_PITCH
    h1 = n1 // 2

    @pl.when(j == 0)
    def _():
        def load(a, c):
            src = pl.multiple_of(a * SLAB, SLAB)
            dst = pl.multiple_of(a * pitch, 8)
            bufr[pl.ds(dst, SLAB), :] = z_ref[0, pl.ds(src, SLAB), :].astype(F32)
            bufi[pl.ds(dst, SLAB), :] = z_ref[1, pl.ds(src, SLAB), :].astype(F32)
            return c
        lax.fori_loop(0, h1, load, 0)
        fa = fa_ref[...]

        def stage_a(b, c):
            x = jnp.concatenate([bufr[pl.ds(b, h1, stride=pitch), :],
                                 bufi[pl.ds(b, h1, stride=pitch), :]], axis=0).astype(BF16)
            y = jnp.dot(fa, x, preferred_element_type=F32)
            bufr[pl.ds(b, n1, stride=pitch), :] = y[:n1]
            bufi[pl.ds(b, n1, stride=pitch), :] = y[n1:]
            return c
        lax.fori_loop(0, SLAB, stage_a, 0, unroll=4)
        _twiddle_init(t_ref)

    fb = fb_ref[...]
    fbi = fbi_ref[...]

    def stage_b(s, c):
        k1 = j * sb + s
        row = pl.multiple_of(k1 * pitch, 8)
        tr, ti = t_ref[0], t_ref[1]
        yr, yi = _cmul(bufr[pl.ds(row, SLAB), :], bufi[pl.ds(row, SLAB), :], tr, ti)
        spec = jnp.dot(fb, jnp.concatenate([yr, yi], axis=0).astype(BF16), preferred_element_type=F32)
        krow = pl.multiple_of(s * SLAB, SLAB)
        pr, pi = _cmul(spec[:SLAB], spec[SLAB:], ks_ref[0, pl.ds(krow, SLAB), :], ks_ref[1, pl.ds(krow, SLAB), :])
        back = jnp.dot(fbi, jnp.concatenate([pr, pi], axis=0).astype(BF16), preferred_element_type=F32)
        ur, ui = _cmul(back[:SLAB], back[SLAB:], tr, -ti)
        bufr[pl.ds(row, SLAB), :] = ur
        bufi[pl.ds(row, SLAB), :] = ui
        _twiddle_step(t_ref, tw_ref, tr, ti)
        return c
    lax.fori_loop(0, sb, stage_b, 0, unroll=4)

    @pl.when(j == kb - 1)
    def _():
        fai = fai_ref[...]

        def stage_ai(b, c):
            x = jnp.concatenate([bufr[pl.ds(b, n1, stride=pitch), :],
                                 bufi[pl.ds(b, n1, stride=pitch), :]], axis=0).astype(BF16)
            y = jnp.dot(fai, x, preferred_element_type=F32)
            bufr[pl.ds(b, h1, stride=pitch), :] = y[:h1]
            bufi[pl.ds(b, h1, stride=pitch), :] = y[h1:]
            return c
        lax.fori_loop(0, SLAB, stage_ai, 0, unroll=4)
        skip = skip_ref[...]

        def store(a, c):
            dst = pl.multiple_of(a * SLAB, SLAB)
            src = pl.multiple_of(a * pitch, 8)
            for bi, buf in enumerate((bufr, bufi)):
                zz = z_ref[bi, pl.ds(dst, SLAB), :].astype(F32)
                gg = g_ref[bi, pl.ds(dst, SLAB), :].astype(F32)
                o_ref[bi, pl.ds(dst, SLAB), :] = (gg * (buf[pl.ds(src, SLAB), :] + zz * skip)).astype(o_ref.dtype)
            return c
        lax.fori_loop(0, h1, store, 0)


def fftconv_gated(z, gate, kspec, skip, out_dtype=BF16):
    bsz, length, d = z.shape
    assert bsz == 2
    n1 = 2 * length // SLAB
    kb = _slab_blocks(n1)
    sb = n1 // kb
    c = _two_stage_consts(n1, SLAB)
    fa = jnp.asarray(c["fa_half"], BF16)
    fb = jnp.asarray(c["fb"], BF16)
    fbi = jnp.asarray(c["fbi"], BF16)
    fai = jnp.asarray(c["fai"], BF16)
    tw = jnp.asarray(c["tw"], F32)
    const = lambda shape: pl.BlockSpec(shape, lambda i, j: (0,) * len(shape))
    seq = pl.BlockSpec((2, length, LANES), lambda i, j: (0, 0, i))
    return pl.pallas_call(
        functools.partial(_fftconv_kernel, n1=n1, sb=sb, kb=kb),
        grid=(d // LANES, kb),
        in_specs=[seq, seq,
                  pl.BlockSpec((2, sb * SLAB, LANES), lambda i, j: (0, j, i)),
                  pl.BlockSpec((1, LANES), lambda i, j: (0, i)),
                  const(fa.shape), const(fb.shape), const(fbi.shape), const(fai.shape), const(tw.shape)],
        out_specs=seq,
        out_shape=jax.ShapeDtypeStruct((2, length, d), out_dtype),
        scratch_shapes=[pltpu.VMEM((n1 * SLAB_PITCH, LANES), F32),
                        pltpu.VMEM((n1 * SLAB_PITCH, LANES), F32),
                        pltpu.VMEM((2, SLAB, LANES), F32)],
        compiler_params=_cparams(("parallel", "arbitrary")),
        name="fftconv",
    )(z, gate, kspec, skip.reshape(1, d), fa, fb, fbi, fai, tw)


def _dense_conv_kernel(z_ref, g_ref, k_ref, skip_ref, fwd_ref, fwdk_ref, inv_ref, o_ref, *, length):
    n = 2 * length
    x = jnp.concatenate([z_ref[0], z_ref[1]], axis=0)
    spec = jnp.dot(fwd_ref[...], x.astype(BF16), preferred_element_type=F32)
    kspec = jnp.dot(fwdk_ref[...], k_ref[...].astype(BF16), preferred_element_type=F32)
    pr, pi = _cmul(spec[:n], spec[n:], kspec[:n], kspec[n:])
    y = jnp.dot(inv_ref[...], jnp.concatenate([pr, pi], axis=0).astype(BF16), preferred_element_type=F32)
    skip = skip_ref[...]
    for bi in range(2):
        yy = y[bi * length:(bi + 1) * length]
        o_ref[bi] = (g_ref[bi].astype(F32) * (yy + z_ref[bi].astype(F32) * skip)).astype(o_ref.dtype)


def dense_conv_gated(z, gate, k_time, skip, out_dtype=BF16):
    bsz, length, d = z.shape
    assert bsz == 2
    c = _dense_consts(length)
    fwd = jnp.asarray(c["fwd"], BF16)
    fwdk = jnp.asarray(c["fwd_real"], BF16)
    inv = jnp.asarray(c["inv"], BF16)
    tile = 2 * LANES if d % (2 * LANES) == 0 else LANES
    const = lambda shape: pl.BlockSpec(shape, lambda i: (0,) * len(shape))
    seq = pl.BlockSpec((2, length, tile), lambda i: (0, 0, i))
    return pl.pallas_call(
        functools.partial(_dense_conv_kernel, length=length),
        grid=(d // tile,),
        in_specs=[seq, seq, pl.BlockSpec((2 * length, tile), lambda i: (0, i)),
                  pl.BlockSpec((1, tile), lambda i: (0, i)),
                  const(fwd.shape), const(fwdk.shape), const(inv.shape)],
        out_specs=seq,
        out_shape=jax.ShapeDtypeStruct((2, length, d), out_dtype),
        compiler_params=_cparams(("parallel",)),
        name="dense_conv",
    )(z, gate, k_time, skip.reshape(1, d), fwd, fwdk, inv)


def _bdot(a, b):
    return lax.dot_general(a.astype(BF16), b.astype(BF16), (((2,), (1,)), ((0,), (0,))),
                           preferred_element_type=F32)


def _bdot_nt(a, b):
    return lax.dot_general(a.astype(BF16), b.astype(BF16), (((2,), (2,)), ((0,), (0,))),
                           preferred_element_type=F32)


def _dot(a, b):
    return jnp.dot(a.astype(BF16), b.astype(BF16), preferred_element_type=F32)


def _dot_tn(a, b):
    return lax.dot_general(a.astype(BF16), b.astype(BF16), (((0,), (0,)), ((), ())), preferred_element_type=F32)


def _gdn_kernel(qf_ref, kf_ref, vf_ref, qb_ref, kb_ref, vb_ref, gcf_ref, bcf_ref, gcb_ref, bcb_ref,
                grf_ref, grb_ref, s0_ref, of_ref, ob_ref, sout_ref,
                s_ref, u_ref, w_ref, qg_ref, at_ref, gl_ref, mw_ref, n_ref, sc_ref, *, r, nb, hb, nh):
    cs = GDN_CHUNK
    head0 = pl.program_id(1) * hb
    j = pl.program_id(2)

    @pl.when(j == 0)
    def _():
        s_ref[...] = s0_ref[0]

    ii = lax.broadcasted_iota(jnp.int32, (cs, cs), 0)
    jj = lax.broadcasted_iota(jnp.int32, (cs, cs), 1)
    eye = (ii == jj).astype(F32)
    blk_same = [jnp.right_shift(ii, lvl) == jnp.right_shift(jj, lvl) for lvl in range(int(math.log2(cs)) + 1)]
    lane_head = lax.broadcasted_iota(jnp.int32, (r * cs, nh), 1)

    for dr in range(2):
        q_ref, k_ref, v_ref = (qf_ref, kf_ref, vf_ref) if dr == 0 else (qb_ref, kb_ref, vb_ref)
        gcol_ref, bcol_ref = (gcf_ref, bcf_ref) if dr == 0 else (gcb_ref, bcb_ref)
        grow_ref = grf_ref if dr == 0 else grb_ref
        incl = (ii >= jj) if dr == 0 else (ii <= jj)
        strict = (ii > jj) if dr == 0 else (ii < jj)
        last = cs - 1 if dr == 0 else 0
        for hh in range(hb):
            lanes = slice(hh * LANES, (hh + 1) * LANES)
            q = q_ref[0, :, lanes].astype(F32).reshape(r, cs, LANES)
            k = k_ref[0, :, lanes].astype(F32).reshape(r, cs, LANES)
            v = v_ref[0, :, lanes].astype(F32).reshape(r, cs, LANES)
            pick = lane_head == head0 + hh
            gc = jnp.sum(jnp.where(pick, gcol_ref[0, 0], 0.0), axis=1, keepdims=True).reshape(r, cs, 1)
            bc = jnp.sum(jnp.where(pick, bcol_ref[0, 0], 0.0), axis=1, keepdims=True).reshape(r, cs, 1)
            gr = grow_ref[0, 0, hh]
            g_last = gr[:, :, last:last + 1]
            decay = jnp.where(incl, jnp.exp(jnp.where(incl, gc - gr, 0.0)), 0.0)
            kbeta = k * bc
            a_mat = jnp.where(strict, _bdot_nt(kbeta, k) * decay, 0.0).astype(BF16).astype(F32)
            t = eye - jnp.where(blk_same[1] & ~blk_same[0], a_mat, 0.0)
            for lvl in range(1, len(blk_same) - 1):
                x = jnp.where(blk_same[lvl + 1] & ~blk_same[lvl], a_mat, 0.0)
                t = t - _bdot(t, _bdot(x, t))
            t_hi = t.astype(BF16)
            a_t = _bdot(a_mat, t_hi) + _bdot(a_mat, t - t_hi.astype(F32))
            t = t + _bdot(t, eye - t - a_t)
            eg = jnp.exp(gc)
            sol = _bdot(t, jnp.concatenate([v * bc, kbeta * eg], axis=2))
            u, w = sol[:, :, :LANES], sol[:, :, LANES:]
            u_ref[dr, hh] = u
            w_ref[dr, hh] = w
            at_ref[dr, hh] = jnp.where(incl, _bdot_nt(q, k) * decay, 0.0)
            qg_ref[dr, hh] = q * eg
            kg_t = jnp.swapaxes(k * jnp.exp(g_last - gc), 1, 2)
            mw_ref[dr, hh] = -_bdot(kg_t, w)
            n_ref[dr, hh] = _bdot(kg_t, u)
            gl_ref[dr, hh] = jnp.broadcast_to(jnp.exp(g_last), (r, 8, LANES))

    def scan(p, carry):
        for dr in range(2):
            c = p if dr == 0 else r - 1 - p
            for hh in range(hb):
                s = s_ref[dr, hh]
                sc_ref[dr, hh, c] = s
                s_ref[dr, hh] = s * gl_ref[dr, hh, c][0:1, :] + _dot(mw_ref[dr, hh, c], s) + n_ref[dr, hh, c]
        return carry
    lax.fori_loop(0, r, scan, 0, unroll=2)

    for dr in range(2):
        o_ref = of_ref if dr == 0 else ob_ref
        for hh in range(hb):
            sc = sc_ref[dr, hh]
            v_new = u_ref[dr, hh] - _bdot(w_ref[dr, hh], sc)
            o = _bdot(qg_ref[dr, hh], sc) + _bdot(at_ref[dr, hh], v_new)
            o_ref[0, :, hh * LANES:(hh + 1) * LANES] = o.reshape(r * cs, LANES)

    @pl.when(j == nb - 1)
    def _():
        sout_ref[0] = s_ref[...]


def gdn_scan(qkv, gcol, bcol, grow, s0, nh):
    bsz, length, d3 = qkv.shape
    d = d3 // 3
    dh = d // nh
    assert dh == LANES
    hb = 2 if nh % 2 == 0 else 1
    ng = nh // hb
    nch = length // GDN_CHUNK
    r = 8 if nch % 8 == 0 else nch
    nb = nch // r
    rows = r * GDN_CHUNK
    blk = lambda j, rev: nb - 1 - j if rev else j
    seq = lambda off, rev: pl.BlockSpec((1, rows, hb * LANES), lambda b, g, j: (b, blk(j, rev), off * ng + g))
    col = lambda rev: pl.BlockSpec((1, 1, rows, nh), lambda b, g, j: (b, rev, blk(j, rev), 0))
    row = lambda rev: pl.BlockSpec((1, 1, hb, r, 1, GDN_CHUNK), lambda b, g, j: (b, rev, g, blk(j, rev), 0, 0))
    out = lambda rev: pl.BlockSpec((1, rows, hb * LANES), lambda b, g, j: (b, blk(j, rev), g))
    state = pl.BlockSpec((1, 2, hb, dh, dh), lambda b, g, j: (b, 0, g, 0, 0))
    per_chunk = lambda *shape: pltpu.VMEM((2, hb, r) + shape, F32)
    return pl.pallas_call(
        functools.partial(_gdn_kernel, r=r, nb=nb, hb=hb, nh=nh),
        grid=(bsz, ng, nb),
        in_specs=[seq(0, 0), seq(1, 0), seq(2, 0), seq(0, 1), seq(1, 1), seq(2, 1),
                  col(0), col(0), col(1), col(1), row(0), row(1), state],
        out_specs=[out(0), out(1), state],
        out_shape=[jax.ShapeDtypeStruct((bsz, length, d), F32),
                   jax.ShapeDtypeStruct((bsz, length, d), F32),
                   jax.ShapeDtypeStruct((bsz, 2, nh, dh, dh), F32)],
        scratch_shapes=[pltpu.VMEM((2, hb, dh, dh), F32),
                        per_chunk(GDN_CHUNK, LANES), per_chunk(GDN_CHUNK, LANES), per_chunk(GDN_CHUNK, LANES),
                        per_chunk(GDN_CHUNK, GDN_CHUNK), per_chunk(8, LANES),
                        per_chunk(dh, dh), per_chunk(dh, dh), per_chunk(dh, dh)],
        compiler_params=_cparams(("parallel", "parallel", "arbitrary")),
        name="gdn_scan",
    )(qkv, qkv, qkv, qkv, qkv, qkv, gcol, bcol, gcol, bcol, grow, grow, s0)


def _layer_norm(x, g, b):
    mu = jnp.mean(x, axis=-1, keepdims=True)
    var = jnp.mean(jnp.square(x - mu), axis=-1, keepdims=True)
    return (x - mu) * lax.rsqrt(var + LN_EPS) * g + b


def _post_norm(x, y, gate, g, b):
    return _layer_norm(DN_ALPHA * x + gate * y, g, b)


def _modulate(x, shift, scale):
    return x * (1 + scale) + shift


def _dwconv1d(x, w):
    zero = jnp.zeros_like(x[:, :1])
    prev = jnp.concatenate([zero, x[:, :-1]], axis=1)
    nxt = jnp.concatenate([x[:, 1:], zero], axis=1)
    return prev * w[0] + x * w[1] + nxt * w[2]


def _dwconv2d_grid(x, w, b, rows, cols):
    bsz, n_tok, ch = x.shape
    img = jnp.pad(x.reshape(bsz, rows, cols, ch), ((0, 0), (1, 1), (1, 1), (0, 0)))
    y = sum(img[:, di:di + rows, dj:dj + cols] * w[di, dj] for di in range(3) for dj in range(3))
    return y.reshape(bsz, n_tok, ch) + b


def _proj(h, w, bias=None, out_dtype=F32, **tiles):
    bsz, length, kd = h.shape
    return matmul(h.reshape(bsz * length, kd).astype(BF16), w, bias, out_dtype, **tiles).reshape(bsz, length, -1)


def _hyena_filter_times(length, p, d):
    t = jnp.linspace(0.0, 1.0, length, dtype=F32)[:, None]
    bands = (HY_EMB_DIM - 1) // 2
    ang = 2.0 * math.pi * jnp.arange(length, dtype=F32)[:, None] / length
    f = jnp.linspace(1e-4, bands - 1, bands, dtype=F32)[None, :]
    z = jnp.concatenate([t, jnp.cos(f * ang), -jnp.sin(f * ang)], axis=-1)
    freq = p['hy_f_freq']
    h = jnp.sin(freq * (z @ p['hy_f_w1'] + p['hy_f_b1']))
    h = jnp.sin(freq * (h @ p['hy_f_w2'] + p['hy_f_b2']))
    h = jnp.sin(freq * (h @ p['hy_f_w3'] + p['hy_f_b3']))
    w_out = p['hy_f_wout'].reshape(-1, 2 * HY_ORDER, d)
    max_decay = math.log(HY_DECAY_TARGET) / HY_FAST_DECAY_PCT
    min_decay = math.log(HY_DECAY_TARGET) / HY_SLOW_DECAY_PCT
    deltas = jnp.abs(jnp.linspace(min_decay, max_decay, d, dtype=F32))
    h_rev, t_rev = h[::-1], t[::-1]
    out = []
    for o in range(HY_ORDER):
        hf = matmul(h, w_out[:, 2 * o]) * jnp.exp(-t * deltas)
        hb_rev = matmul(h_rev, w_out[:, 2 * o + 1]) * jnp.exp(-t_rev * deltas)
        out.append(jnp.concatenate([hf[:1] + hb_rev[-1:], hf[1:], jnp.zeros_like(hf[:1]), hb_rev[:-1]], axis=0))
    return out


def _hyena_mixer(h, p, w_in, w_out):
    bsz, length, d = h.shape
    proj = _proj(h, w_in, p['hy_b_in'], out_dtype=BF16).astype(F32)
    u = (_dwconv1d(proj, p['hy_w_short']) + p['hy_b_short']).astype(BF16)
    parts = jnp.split(u, HY_ORDER + 1, axis=-1)
    k_times = _hyena_filter_times(length, p, d)
    z = parts[-1]
    two_stage = (2 * length) % (16 * SLAB) == 0
    for o in range(HY_ORDER):
        if two_stage:
            z = fftconv_gated(z, parts[o], filter_spectrum(k_times[o]), p['hy_skip'][o])
        else:
            z = dense_conv_gated(z, parts[o], k_times[o], p['hy_skip'][o])
    return _proj(z, w_out, p['hy_b_out'])


def _gdn_mixer(h, p, w_qkvz, w_ab, w_out, s0):
    bsz, length, d = h.shape
    nh = p['gdn_a_log'].shape[1]
    dh = d // nh
    proj = _proj(h, w_qkvz, out_dtype=BF16).astype(F32)
    ab = _proj(h, w_ab).reshape(bsz, length, 2, 2, nh)
    qkv = jax.nn.silu(_dwconv1d(proj[..., :3 * d], p['gdn_w_conv']))
    z_gate = proj[..., 3 * d:].reshape(bsz, length, nh, dh)

    def l2n(t, scale):
        t = t.reshape(bsz, length, nh, dh)
        t = t * lax.rsqrt(jnp.sum(t * t, axis=-1, keepdims=True) + L2_EPS) * scale
        return t.reshape(bsz, length, d)

    q, k, v = jnp.split(qkv, 3, axis=-1)
    qkv = jnp.concatenate([l2n(q, dh ** -0.5), l2n(k, 1.0), v], axis=-1).astype(BF16)
    g = -jnp.exp(p['gdn_a_log']) * jax.nn.softplus(ab[:, :, 0] + p['gdn_dt_bias'])
    beta = jax.nn.sigmoid(ab[:, :, 1]).transpose(0, 2, 1, 3)
    nch = length // GDN_CHUNK
    g = g.transpose(0, 2, 1, 3).reshape(bsz, 2, nch, GDN_CHUNK, nh)
    g = jnp.stack([jnp.cumsum(g[:, 0], axis=2),
                   jnp.flip(jnp.cumsum(jnp.flip(g[:, 1], axis=2), axis=2), axis=2)], axis=1)
    gcol = g.reshape(bsz, 2, length, nh)
    grow = g.transpose(0, 1, 4, 2, 3)[:, :, :, :, None, :]
    o_f, o_b, s_out = gdn_scan(qkv, gcol, beta, grow, s0, nh)
    o = (o_f + o_b).reshape(bsz, length, nh, dh)
    o = o * lax.rsqrt(jnp.mean(o * o, axis=-1, keepdims=True) + RMS_EPS) * p['gdn_norm_w'] * jax.nn.silu(z_gate)
    return _proj(o.reshape(bsz, length, d), w_out), s_out


FFN_PAD = 1024


def _ffn_weights(p):
    f = p['ffn_b_dw'].shape[0]
    fp = -(-f // FFN_PAD) * FFN_PAD if f > FFN_PAD else f
    pad = lambda a, axis: jnp.pad(a, [(0, fp - f) if ax == axis else (0, 0) for ax in range(a.ndim)])
    w_up = p['ffn_w_up'].astype(BF16)
    w_up = jnp.concatenate([pad(w_up[:, :f], 1), pad(w_up[:, f:], 1)], axis=1)
    return w_up, pad(p['ffn_w_down'].astype(BF16), 0), pad(p['ffn_w_dw'], 2), pad(p['ffn_b_dw'], 0)


def _conv_glu(h, ffn_w, rows, cols):
    w_up, w_down, w_dw, b_dw = ffn_w
    fp = b_dw.shape[0]
    up = _proj(h, w_up, out_dtype=BF16)
    gate = _dwconv2d_grid(up[..., :fp].astype(F32), w_dw, b_dw, rows, cols)
    act = jax.nn.gelu(gate, approximate=False) * up[..., fp:]
    return _proj(act, w_down)


def kernel(x, c, ctx, c_ctx, l0_w_ada, l0_b_ada, l0_ln1_g, l0_ln1_b, l0_ln2_g, l0_ln2_b, l0_hy_w_in, l0_hy_b_in, l0_hy_w_short, l0_hy_b_short, l0_hy_f_w1, l0_hy_f_b1, l0_hy_f_w2, l0_hy_f_b2, l0_hy_f_w3, l0_hy_f_b3, l0_hy_f_wout, l0_hy_f_freq, l0_hy_skip, l0_hy_w_out, l0_hy_b_out, l0_ffn_w_up, l0_ffn_w_dw, l0_ffn_b_dw, l0_ffn_w_down, l1_w_ada, l1_b_ada, l1_ln1_g, l1_ln1_b, l1_ln2_g, l1_ln2_b, l1_gdn_w_in, l1_gdn_w_conv, l1_gdn_a_log, l1_gdn_dt_bias, l1_gdn_norm_w, l1_gdn_w_out, l1_ffn_w_up, l1_ffn_w_dw, l1_ffn_b_dw, l1_ffn_w_down):
    layers = (
        dict(w_ada=l0_w_ada, b_ada=l0_b_ada, ln1_g=l0_ln1_g, ln1_b=l0_ln1_b, ln2_g=l0_ln2_g, ln2_b=l0_ln2_b,
             hy_w_in=l0_hy_w_in, hy_b_in=l0_hy_b_in, hy_w_short=l0_hy_w_short, hy_b_short=l0_hy_b_short,
             hy_f_w1=l0_hy_f_w1, hy_f_b1=l0_hy_f_b1, hy_f_w2=l0_hy_f_w2, hy_f_b2=l0_hy_f_b2,
             hy_f_w3=l0_hy_f_w3, hy_f_b3=l0_hy_f_b3, hy_f_wout=l0_hy_f_wout, hy_f_freq=l0_hy_f_freq,
             hy_skip=l0_hy_skip, hy_w_out=l0_hy_w_out, hy_b_out=l0_hy_b_out,
             ffn_w_up=l0_ffn_w_up, ffn_w_dw=l0_ffn_w_dw, ffn_b_dw=l0_ffn_b_dw, ffn_w_down=l0_ffn_w_down),
        dict(w_ada=l1_w_ada, b_ada=l1_b_ada, ln1_g=l1_ln1_g, ln1_b=l1_ln1_b, ln2_g=l1_ln2_g, ln2_b=l1_ln2_b,
             gdn_w_in=l1_gdn_w_in, gdn_w_conv=l1_gdn_w_conv, gdn_a_log=l1_gdn_a_log,
             gdn_dt_bias=l1_gdn_dt_bias, gdn_norm_w=l1_gdn_norm_w, gdn_w_out=l1_gdn_w_out,
             ffn_w_up=l1_ffn_w_up, ffn_w_dw=l1_ffn_w_dw, ffn_b_dw=l1_ffn_b_dw, ffn_w_down=l1_ffn_w_down),
    )
    bsz, seq, d = x.shape
    rows = seq // GRID_W
    ctx_len = ctx.shape[1]
    cond = jnp.concatenate([c, c_ctx[None], jnp.zeros((8 - bsz - 1, d), F32)], axis=0)
    for i, p in enumerate(layers):
        last = i == DEPTH - 1
        mod = matmul(jax.nn.silu(cond), p['w_ada'], p['b_ada'], tn=512)
        sh1, sc1, gt1, sh2, sc2, gt2 = jnp.split(mod[:bsz, None, :], 6, axis=-1)
        csh1, csc1, cgt1, csh2, csc2, cgt2 = jnp.split(mod[bsz:bsz + 1], 6, axis=-1)
        h_lat = _modulate(x, sh1, sc1).astype(BF16)
        h_ctx = _modulate(ctx, csh1, csc1).astype(BF16)
        if 'hy_w_in' in p:
            w_in, w_out = p['hy_w_in'].astype(BF16), p['hy_w_out'].astype(BF16)
            y_lat = _hyena_mixer(h_lat, p, w_in, w_out)
            y_ctx = None if last else _hyena_mixer(h_ctx, p, w_in, w_out)
        else:
            nh = p['gdn_a_log'].shape[1]
            w_qkvz = p['gdn_w_in'][:, :4 * d].astype(BF16)
            w_ab = p['gdn_w_in'][:, 4 * d:].astype(BF16)
            w_out = p['gdn_w_out'].astype(BF16)
            s0 = jnp.zeros((bsz, 2, nh, d // nh, d // nh), F32)
            y_ctx, s_ctx = _gdn_mixer(h_ctx, p, w_qkvz, w_ab, w_out, s0)
            y_lat, _ = _gdn_mixer(h_lat, p, w_qkvz, w_ab, w_out, s_ctx)
        ffn_w = _ffn_weights(p)
        x = _post_norm(x, y_lat, gt1, p['ln1_g'], p['ln1_b'])
        ffn = _conv_glu(_modulate(x, sh2, sc2).astype(BF16), ffn_w, rows, GRID_W)
        x = _post_norm(x, ffn, gt2, p['ln2_g'], p['ln2_b'])
        if not last:
            ctx = _post_norm(ctx, y_ctx, cgt1, p['ln1_g'], p['ln1_b'])
            ffn = _conv_glu(_modulate(ctx, csh2, csc2).astype(BF16), ffn_w, 1, ctx_len)
            ctx = _post_norm(ctx, ffn, cgt2, p['ln2_g'], p['ln2_b'])
    return x
```

```python
import functools
import math

import numpy as np
import jax
import jax.numpy as jnp
from jax import lax
from jax.experimental import pallas as pl
from jax.experimental.pallas import tpu as pltpu

F32 = jnp.float32
BF16 = jnp.bfloat16

GRID_W = 64
HY_ORDER = 2
HY_EMB_DIM = 33
HY_DECAY_TARGET = 1e-2
HY_FAST_DECAY_PCT = 0.3
HY_SLOW_DECAY_PCT = 1.5
GDN_CHUNK = 64
LN_EPS = 1e-5
RMS_EPS = 1e-6
L2_EPS = 1e-6
DEPTH = 2
DN_ALPHA = (2 * DEPTH) ** 0.25

LANES = 128
SLAB = 128
SLAB_PITCH = SLAB + 8
VMEM_LIMIT = 56 * 1024 * 1024


def _cparams(sem):
    return pltpu.CompilerParams(dimension_semantics=sem, vmem_limit_bytes=VMEM_LIMIT)


def _mm_kernel(a_ref, w_ref, b_ref, o_ref, acc_ref, *, nk):
    part = jnp.dot(a_ref[...].astype(BF16), w_ref[...].astype(BF16), preferred_element_type=F32)
    if nk == 1:
        o_ref[...] = (part + b_ref[...]).astype(o_ref.dtype)
        return
    k = pl.program_id(2)

    @pl.when(k == 0)
    def _():
        acc_ref[...] = part

    @pl.when(k > 0)
    def _():
        acc_ref[...] += part

    @pl.when(k == nk - 1)
    def _():
        o_ref[...] = (acc_ref[...] + b_ref[...]).astype(o_ref.dtype)


def _pick(n, pref):
    if n <= pref:
        return n
    t = (pref // LANES) * LANES
    while t >= LANES:
        if n % t == 0:
            return t
        t -= LANES
    return n


def matmul(a, w, bias=None, out_dtype=F32, tm=1024, tn=1024, tk=4096):
    m, kd = a.shape
    kd2, n = w.shape
    assert kd == kd2
    tm, tn, tk = _pick(m, tm), _pick(n, tn), _pick(kd, tk)
    nk = kd // tk
    if bias is None:
        bias = jnp.zeros((n,), F32)
    bias = bias.reshape(1, n).astype(F32)
    return pl.pallas_call(
        functools.partial(_mm_kernel, nk=nk),
        grid=(m // tm, n // tn, nk),
        in_specs=[pl.BlockSpec((tm, tk), lambda i, j, k: (i, k)),
                  pl.BlockSpec((tk, tn), lambda i, j, k: (k, j)),
                  pl.BlockSpec((1, tn), lambda i, j, k: (0, j))],
        out_specs=pl.BlockSpec((tm, tn), lambda i, j, k: (i, j)),
        out_shape=jax.ShapeDtypeStruct((m, n), out_dtype),
        scratch_shapes=[pltpu.VMEM((tm, tn) if nk > 1 else (8, LANES), F32)],
        compiler_params=_cparams(("parallel", "parallel", "arbitrary")),
        name="matmul",
    )(a, w, bias)


def _real_form(mat):
    return np.block([[mat.real, -mat.imag], [mat.imag, mat.real]])


@functools.lru_cache(maxsize=None)
def _two_stage_consts(n1, n2):
    n = n1 * n2
    h1 = n1 // 2
    a = np.arange(n1)
    b = np.arange(n2)
    ma = np.exp(-2j * np.pi * np.outer(a, a) / n1)
    mb = np.exp(-2j * np.pi * np.outer(b, b) / n2)
    fa_half = _real_form(ma[:, :h1])
    fa_real = np.concatenate([ma.real, ma.imag], axis=0)
    fb = _real_form(mb)
    fbi = _real_form(np.conj(mb))
    fai = _real_form(np.conj(ma)[:h1, :] / n)
    tw = np.exp(-2j * np.pi * b / n)
    tw = np.stack([np.broadcast_to(tw.real[:, None], (n2, LANES)),
                   np.broadcast_to(tw.imag[:, None], (n2, LANES))])
    return dict(fa_half=fa_half, fa_real=fa_real, fb=fb, fbi=fbi, fai=fai, tw=tw)


@functools.lru_cache(maxsize=None)
def _dense_consts(length):
    n = 2 * length
    f = np.arange(n)
    t = np.arange(length)
    fwd = np.exp(-2j * np.pi * np.outer(f, t) / n)
    fwd_full = np.exp(-2j * np.pi * np.outer(f, f) / n)
    inv = np.exp(2j * np.pi * np.outer(t, f) / n) / n
    return dict(fwd=_real_form(fwd),
                fwd_real=np.concatenate([fwd_full.real, fwd_full.imag], axis=0),
                inv=_real_form(inv))


def _cmul(ar, ai, br, bi):
    return ar * br - ai * bi, ar * bi + ai * br


def _twiddle_init(t_ref):
    t_ref[0] = jnp.ones(t_ref.shape[1:], F32)
    t_ref[1] = jnp.zeros(t_ref.shape[1:], F32)


def _twiddle_step(t_ref, tw_ref, tr, ti):
    nr, ni = _cmul(tr, ti, tw_ref[0], tw_ref[1])
    t_ref[0] = nr
    t_ref[1] = ni


def _filter_spectrum_kernel(hf_ref, hb_ref, k0_ref, delta_ref, fa_ref, fb_ref, tw_ref, o_ref, bufr, bufi, t_ref,
                            *, n1, sb):
    j = pl.program_id(1)
    pitch = SLAB_PITCH
    h1 = n1 // 2
    length = h1 * SLAB

    @pl.when(j == 0)
    def _():
        rate = delta_ref[...] * (1.0 / (length - 1))
        pos = lax.broadcasted_iota(jnp.int32, (SLAB, LANES), 0).astype(F32)

        def load(a, c):
            src = pl.multiple_of(a * SLAB, SLAB)
            m = pos + (a * SLAB).astype(F32)
            bufr[pl.ds(pl.multiple_of(a * pitch, 8), SLAB), :] = hf_ref[pl.ds(src, SLAB), :] * jnp.exp(-m * rate)
            bufr[pl.ds(pl.multiple_of((h1 + a) * pitch, 8), SLAB), :] = (
                hb_ref[pl.ds(src, SLAB), :] * jnp.exp((m - length) * rate))
            return c
        lax.fori_loop(0, h1, load, 0)
        bufr[0:1, :] = bufr[0:1, :] + k0_ref[...]
        fa = fa_ref[...]

        def stage_a(b, c):
            x = bufr[pl.ds(b, n1, stride=pitch), :].astype(BF16)
            y = jnp.dot(fa, x, preferred_element_type=F32)
            bufr[pl.ds(b, n1, stride=pitch), :] = y[:n1]
            bufi[pl.ds(b, n1, stride=pitch), :] = y[n1:]
            return c
        lax.fori_loop(0, SLAB, stage_a, 0, unroll=4)
        _twiddle_init(t_ref)

    fb = fb_ref[...]

    def stage_b(s, c):
        k1 = j * sb + s
        row = pl.multiple_of(k1 * pitch, 8)
        tr, ti = t_ref[0], t_ref[1]
        yr, yi = _cmul(bufr[pl.ds(row, SLAB), :], bufi[pl.ds(row, SLAB), :], tr, ti)
        x = jnp.concatenate([yr, yi], axis=0).astype(BF16)
        spec = jnp.dot(fb, x, preferred_element_type=F32)
        out = pl.multiple_of(s * SLAB, SLAB)
        o_ref[0, pl.ds(out, SLAB), :] = spec[:SLAB]
        o_ref[1, pl.ds(out, SLAB), :] = spec[SLAB:]
        _twiddle_step(t_ref, tw_ref, tr, ti)
        return c
    lax.fori_loop(0, sb, stage_b, 0, unroll=4)


def _slab_blocks(n1):
    return 8 if n1 % 8 == 0 and n1 >= 16 else 1


def filter_spectrum(hf, hb_shift, k0, deltas):
    length, d = hf.shape
    n = 2 * length
    n1 = n // SLAB
    kb = _slab_blocks(n1)
    sb = n1 // kb
    c = _two_stage_consts(n1, SLAB)
    fa = jnp.asarray(c["fa_real"], BF16)
    fb = jnp.asarray(c["fb"], BF16)
    tw = jnp.asarray(c["tw"], F32)
    const = lambda shape: pl.BlockSpec(shape, lambda i, j: (0,) * len(shape))
    return pl.pallas_call(
        functools.partial(_filter_spectrum_kernel, n1=n1, sb=sb),
        grid=(d // LANES, kb),
        in_specs=[pl.BlockSpec((length, LANES), lambda i, j: (0, i)),
                  pl.BlockSpec((length, LANES), lambda i, j: (0, i)),
                  pl.BlockSpec((1, LANES), lambda i, j: (0, i)),
                  pl.BlockSpec((1, LANES), lambda i, j: (0, i)),
                  const(fa.shape), const(fb.shape), const(tw.shape)],
        out_specs=pl.BlockSpec((2, sb * SLAB, LANES), lambda i, j: (0, j, i)),
        out_shape=jax.ShapeDtypeStruct((2, n, d), F32),
        scratch_shapes=[pltpu.VMEM((n1 * SLAB_PITCH, LANES), F32),
                        pltpu.VMEM((n1 * SLAB_PITCH, LANES), F32),
                        pltpu.VMEM((2, SLAB, LANES), F32)],
        compiler_params=_cparams(("parallel", "arbitrary")),
        name="filter_spectrum",
    )(hf, hb_shift, k0, deltas.reshape(1, d), fa, fb, tw)


def _fftconv_kernel(z_ref, g_ref, ks_ref, skip_ref, fa_ref, fb_ref, fbi_ref, fai_ref, tw_ref, o_ref,
                    bufr, bufi, t_ref, *, n1, sb, kb):
    j = pl.program_id(1)
    pitch = SLAB_PITCH
    h1 = n1 // 2

    @pl.when(j == 0)
    def _():
        def load(a, c):
            src = pl.multiple_of(a * SLAB, SLAB)
            dst = pl.multiple_of(a * pitch, 8)
            bufr[pl.ds(dst, SLAB), :] = z_ref[0, pl.ds(src, SLAB), :].astype(F32)
            bufi[pl.ds(dst, SLAB), :] = z_ref[1, pl.ds(src, SLAB), :].astype(F32)
            return c
        lax.fori_loop(0, h1, load, 0)
        fa = fa_ref[...]

        def stage_a(b, c):
            x = jnp.concatenate([bufr[pl.ds(b, h1, stride=pitch), :],
                                 bufi[pl.ds(b, h1, stride=pitch), :]], axis=0).astype(BF16)
            y = jnp.dot(fa, x, preferred_element_type=F32)
            bufr[pl.ds(b, n1, stride=pitch), :] = y[:n1]
            bufi[pl.ds(b, n1, stride=pitch), :] = y[n1:]
            return c
        lax.fori_loop(0, SLAB, stage_a, 0, unroll=4)
        _twiddle_init(t_ref)

    fb = fb_ref[...]
    fbi = fbi_ref[...]

    def stage_b(s, c):
        k1 = j * sb + s
        row = pl.multiple_of(k1 * pitch, 8)
        tr, ti = t_ref[0], t_ref[1]
        yr, yi = _cmul(bufr[pl.ds(row, SLAB), :], bufi[pl.ds(row, SLAB), :], tr, ti)
        spec = jnp.dot(fb, jnp.concatenate([yr, yi], axis=0).astype(BF16), preferred_element_type=F32)
        krow = pl.multiple_of(s * SLAB, SLAB)
        pr, pi = _cmul(spec[:SLAB], spec[SLAB:], ks_ref[0, pl.ds(krow, SLAB), :], ks_ref[1, pl.ds(krow, SLAB), :])
        back = jnp.dot(fbi, jnp.concatenate([pr, pi], axis=0).astype(BF16), preferred_element_type=F32)
        ur, ui = _cmul(back[:SLAB], back[SLAB:], tr, -ti)
        bufr[pl.ds(row, SLAB), :] = ur
        bufi[pl.ds(row, SLAB), :] = ui
        _twiddle_step(t_ref, tw_ref, tr, ti)
        return c
    lax.fori_loop(0, sb, stage_b, 0, unroll=4)

    @pl.when(j == kb - 1)
    def _():
        fai = fai_ref[...]

        def stage_ai(b, c):
            x = jnp.concatenate([bufr[pl.ds(b, n1, stride=pitch), :],
                                 bufi[pl.ds(b, n1, stride=pitch), :]], axis=0).astype(BF16)
            y = jnp.dot(fai, x, preferred_element_type=F32)
            bufr[pl.ds(b, h1, stride=pitch), :] = y[:h1]
            bufi[pl.ds(b, h1, stride=pitch), :] = y[h1:]
            return c
        lax.fori_loop(0, SLAB, stage_ai, 0, unroll=4)
        skip = skip_ref[...]

        def store(a, c):
            dst = pl.multiple_of(a * SLAB, SLAB)
            src = pl.multiple_of(a * pitch, 8)
            for bi, buf in enumerate((bufr, bufi)):
                zz = z_ref[bi, pl.ds(dst, SLAB), :].astype(F32)
                gg = g_ref[bi, pl.ds(dst, SLAB), :].astype(F32)
                o_ref[bi, pl.ds(dst, SLAB), :] = (gg * (buf[pl.ds(src, SLAB), :] + zz * skip)).astype(o_ref.dtype)
            return c
        lax.fori_loop(0, h1, store, 0)


def fftconv_gated(z, gate, kspec, skip, z_col=0, gate_col=0, out_dtype=BF16):
    bsz, length, _ = z.shape
    d = skip.shape[0]
    assert bsz == 2
    n1 = 2 * length // SLAB
    kb = _slab_blocks(n1)
    sb = n1 // kb
    c = _two_stage_consts(n1, SLAB)
    fa = jnp.asarray(c["fa_half"], BF16)
    fb = jnp.asarray(c["fb"], BF16)
    fbi = jnp.asarray(c["fbi"], BF16)
    fai = jnp.asarray(c["fai"], BF16)
    tw = jnp.asarray(c["tw"], F32)
    const = lambda shape: pl.BlockSpec(shape, lambda i, j: (0,) * len(shape))
    seq = lambda col=0: pl.BlockSpec((2, length, LANES), lambda i, j: (0, 0, col // LANES + i))
    return pl.pallas_call(
        functools.partial(_fftconv_kernel, n1=n1, sb=sb, kb=kb),
        grid=(d // LANES, kb),
        in_specs=[seq(z_col), seq(gate_col),
                  pl.BlockSpec((2, sb * SLAB, LANES), lambda i, j: (0, j, i)),
                  pl.BlockSpec((1, LANES), lambda i, j: (0, i)),
                  const(fa.shape), const(fb.shape), const(fbi.shape), const(fai.shape), const(tw.shape)],
        out_specs=seq(),
        out_shape=jax.ShapeDtypeStruct((2, length, d), out_dtype),
        scratch_shapes=[pltpu.VMEM((n1 * SLAB_PITCH, LANES), F32),
                        pltpu.VMEM((n1 * SLAB_PITCH, LANES), F32),
                        pltpu.VMEM((2, SLAB, LANES), F32)],
        compiler_params=_cparams(("parallel", "arbitrary")),
        name="fftconv",
    )(z, gate, kspec, skip.reshape(1, d), fa, fb, fbi, fai, tw)


def _dense_conv_kernel(z_ref, g_ref, k_ref, skip_ref, fwd_ref, fwdk_ref, inv_ref, o_ref, *, length):
    n = 2 * length
    x = jnp.concatenate([z_ref[0], z_ref[1]], axis=0)
    spec = jnp.dot(fwd_ref[...], x.astype(BF16), preferred_element_type=F32)
    kspec = jnp.dot(fwdk_ref[...], k_ref[...].astype(BF16), preferred_element_type=F32)
    pr, pi = _cmul(spec[:n], spec[n:], kspec[:n], kspec[n:])
    y = jnp.dot(inv_ref[...], jnp.concatenate([pr, pi], axis=0).astype(BF16), preferred_element_type=F32)
    skip = skip_ref[...]
    for bi in range(2):
        yy = y[bi * length:(bi + 1) * length]
        o_ref[bi] = (g_ref[bi].astype(F32) * (yy + z_ref[bi].astype(F32) * skip)).astype(o_ref.dtype)


def dense_conv_gated(z, gate, k_time, skip, z_col=0, gate_col=0, out_dtype=BF16):
    bsz, length, _ = z.shape
    d = skip.shape[0]
    assert bsz == 2
    c = _dense_consts(length)
    fwd = jnp.asarray(c["fwd"], BF16)
    fwdk = jnp.asarray(c["fwd_real"], BF16)
    inv = jnp.asarray(c["inv"], BF16)
    tile = 2 * LANES if d % (2 * LANES) == 0 else LANES
    const = lambda shape: pl.BlockSpec(shape, lambda i: (0,) * len(shape))
    seq = lambda col=0: pl.BlockSpec((2, length, tile), lambda i: (0, 0, col // tile + i))
    return pl.pallas_call(
        functools.partial(_dense_conv_kernel, length=length),
        grid=(d // tile,),
        in_specs=[seq(z_col), seq(gate_col), pl.BlockSpec((2 * length, tile), lambda i: (0, i)),
                  pl.BlockSpec((1, tile), lambda i: (0, i)),
                  const(fwd.shape), const(fwdk.shape), const(inv.shape)],
        out_specs=seq(),
        out_shape=jax.ShapeDtypeStruct((2, length, d), out_dtype),
        compiler_params=_cparams(("parallel",)),
        name="dense_conv",
    )(z, gate, k_time, skip.reshape(1, d), fwd, fwdk, inv)


def _bdot(a, b):
    return lax.dot_general(a.astype(BF16), b.astype(BF16), (((2,), (1,)), ((0,), (0,))),
                           preferred_element_type=F32)


def _bdot_nt(a, b):
    return lax.dot_general(a.astype(BF16), b.astype(BF16), (((2,), (2,)), ((0,), (0,))),
                           preferred_element_type=F32)


def _dot(a, b):
    return jnp.dot(a.astype(BF16), b.astype(BF16), preferred_element_type=F32)


def _dot_tn(a, b):
    return lax.dot_general(a.astype(BF16), b.astype(BF16), (((0,), (0,)), ((), ())), preferred_element_type=F32)


def _gdn_kernel(qf_ref, kf_ref, vf_ref, qb_ref, kb_ref, vb_ref, gcf_ref, bcf_ref, gcb_ref, bcb_ref,
                grf_ref, grb_ref, s0_ref, of_ref, ob_ref, sout_ref,
                s_ref, u_ref, w_ref, qg_ref, at_ref, gl_ref, mw_ref, n_ref, sc_ref, *, r, nb, hb, nh):
    cs = GDN_CHUNK
    head0 = pl.program_id(1) * hb
    j = pl.program_id(2)

    @pl.when(j == 0)
    def _():
        s_ref[...] = s0_ref[0]

    ii = lax.broadcasted_iota(jnp.int32, (cs, cs), 0)
    jj = lax.broadcasted_iota(jnp.int32, (cs, cs), 1)
    eye = (ii == jj).astype(F32)
    blk_same = [jnp.right_shift(ii, lvl) == jnp.right_shift(jj, lvl) for lvl in range(int(math.log2(cs)) + 1)]
    lane_head = lax.broadcasted_iota(jnp.int32, (r * cs, nh), 1)

    for dr in range(2):
        q_ref, k_ref, v_ref = (qf_ref, kf_ref, vf_ref) if dr == 0 else (qb_ref, kb_ref, vb_ref)
        gcol_ref, bcol_ref = (gcf_ref, bcf_ref) if dr == 0 else (gcb_ref, bcb_ref)
        grow_ref = grf_ref if dr == 0 else grb_ref
        incl = (ii >= jj) if dr == 0 else (ii <= jj)
        strict = (ii > jj) if dr == 0 else (ii < jj)
        last = cs - 1 if dr == 0 else 0
        for hh in range(hb):
            lanes = slice(hh * LANES, (hh + 1) * LANES)
            q = q_ref[0, :, lanes].astype(F32).reshape(r, cs, LANES)
            k = k_ref[0, :, lanes].astype(F32).reshape(r, cs, LANES)
            v = v_ref[0, :, lanes].astype(F32).reshape(r, cs, LANES)
            pick = lane_head == head0 + hh
            gc = jnp.sum(jnp.where(pick, gcol_ref[0, 0], 0.0), axis=1, keepdims=True).reshape(r, cs, 1)
            bc = jnp.sum(jnp.where(pick, bcol_ref[0, 0], 0.0), axis=1, keepdims=True).reshape(r, cs, 1)
            gr = grow_ref[0, 0, hh]
            g_last = gr[:, :, last:last + 1]
            decay = jnp.where(incl, jnp.exp(jnp.where(incl, gc - gr, 0.0)), 0.0)
            kbeta = k * bc
            a_mat = jnp.where(strict, _bdot_nt(kbeta, k) * decay, 0.0).astype(BF16).astype(F32)
            t = eye - jnp.where(blk_same[1] & ~blk_same[0], a_mat, 0.0)
            for lvl in range(1, len(blk_same) - 1):
                x = jnp.where(blk_same[lvl + 1] & ~blk_same[lvl], a_mat, 0.0)
                t = t - _bdot(t, _bdot(x, t))
            t_hi = t.astype(BF16)
            a_t = _bdot(a_mat, t_hi) + _bdot(a_mat, t - t_hi.astype(F32))
            t = t + _bdot(t, eye - t - a_t)
            eg = jnp.exp(gc)
            sol = _bdot(t, jnp.concatenate([v * bc, kbeta * eg], axis=2))
            u, w = sol[:, :, :LANES], sol[:, :, LANES:]
            u_ref[dr, hh] = u
            w_ref[dr, hh] = w
            at_ref[dr, hh] = jnp.where(incl, _bdot_nt(q, k) * decay, 0.0)
            qg_ref[dr, hh] = q * eg
            kg_t = jnp.swapaxes(k * jnp.exp(g_last - gc), 1, 2)
            mw_ref[dr, hh] = -_bdot(kg_t, w)
            n_ref[dr, hh] = _bdot(kg_t, u)
            gl_ref[dr, hh] = jnp.broadcast_to(jnp.exp(g_last), (r, 8, LANES))

    def scan(p, carry):
        for dr in range(2):
            c = p if dr == 0 else r - 1 - p
            for hh in range(hb):
                s = s_ref[dr, hh]
                sc_ref[dr, hh, c] = s
                s_ref[dr, hh] = s * gl_ref[dr, hh, c][0:1, :] + _dot(mw_ref[dr, hh, c], s) + n_ref[dr, hh, c]
        return carry
    lax.fori_loop(0, r, scan, 0, unroll=2)

    for dr in range(2):
        o_ref = of_ref if dr == 0 else ob_ref
        for hh in range(hb):
            sc = sc_ref[dr, hh]
            v_new = u_ref[dr, hh] - _bdot(w_ref[dr, hh], sc)
            o = _bdot(qg_ref[dr, hh], sc) + _bdot(at_ref[dr, hh], v_new)
            o_ref[0, :, hh * LANES:(hh + 1) * LANES] = o.reshape(r * cs, LANES)

    @pl.when(j == nb - 1)
    def _():
        sout_ref[0] = s_ref[...]


def gdn_scan(qkv, gcol, bcol, grow, s0, nh):
    bsz, length, d3 = qkv.shape
    d = d3 // 3
    dh = d // nh
    assert dh == LANES
    hb = 2 if nh % 2 == 0 else 1
    ng = nh // hb
    nch = length // GDN_CHUNK
    r = 8 if nch % 8 == 0 else nch
    nb = nch // r
    rows = r * GDN_CHUNK
    blk = lambda j, rev: nb - 1 - j if rev else j
    seq = lambda off, rev: pl.BlockSpec((1, rows, hb * LANES), lambda b, g, j: (b, blk(j, rev), off * ng + g))
    col = lambda rev: pl.BlockSpec((1, 1, rows, nh), lambda b, g, j: (b, rev, blk(j, rev), 0))
    row = lambda rev: pl.BlockSpec((1, 1, hb, r, 1, GDN_CHUNK), lambda b, g, j: (b, rev, g, blk(j, rev), 0, 0))
    out = lambda rev: pl.BlockSpec((1, rows, hb * LANES), lambda b, g, j: (b, blk(j, rev), g))
    state = pl.BlockSpec((1, 2, hb, dh, dh), lambda b, g, j: (b, 0, g, 0, 0))
    per_chunk = lambda *shape: pltpu.VMEM((2, hb, r) + shape, F32)
    return pl.pallas_call(
        functools.partial(_gdn_kernel, r=r, nb=nb, hb=hb, nh=nh),
        grid=(bsz, ng, nb),
        in_specs=[seq(0, 0), seq(1, 0), seq(2, 0), seq(0, 1), seq(1, 1), seq(2, 1),
                  col(0), col(0), col(1), col(1), row(0), row(1), state],
        out_specs=[out(0), out(1), state],
        out_shape=[jax.ShapeDtypeStruct((bsz, length, d), F32),
                   jax.ShapeDtypeStruct((bsz, length, d), F32),
                   jax.ShapeDtypeStruct((bsz, 2, nh, dh, dh), F32)],
        scratch_shapes=[pltpu.VMEM((2, hb, dh, dh), F32),
                        per_chunk(GDN_CHUNK, LANES), per_chunk(GDN_CHUNK, LANES), per_chunk(GDN_CHUNK, LANES),
                        per_chunk(GDN_CHUNK, GDN_CHUNK), per_chunk(8, LANES),
                        per_chunk(dh, dh), per_chunk(dh, dh), per_chunk(dh, dh)],
        compiler_params=_cparams(("parallel", "parallel", "arbitrary")),
        name="gdn_scan",
    )(qkv, qkv, qkv, qkv, qkv, qkv, gcol, bcol, gcol, bcol, grow, grow, s0)


ROW_TILE = 256
HALO = 16


def _per_batch(v, bsz):
    return jnp.broadcast_to(v.reshape(-1, 1, v.shape[-1]), (bsz, 1, v.shape[-1]))


def _modulate_kernel(x_ref, sh_ref, sc_ref, h_ref):
    h_ref[0] = (x_ref[0] * (1.0 + sc_ref[0]) + sh_ref[0]).astype(h_ref.dtype)


def modulate(x, shift, scale):
    bsz, length, d = x.shape
    tr = _pick(length, ROW_TILE)
    row = pl.BlockSpec((1, tr, d), lambda b, i: (b, i, 0))
    vec = pl.BlockSpec((1, 1, d), lambda b, i: (b, 0, 0))
    return pl.pallas_call(
        _modulate_kernel, grid=(bsz, length // tr), in_specs=[row, vec, vec], out_specs=row,
        out_shape=jax.ShapeDtypeStruct(x.shape, BF16), compiler_params=_cparams(("parallel", "parallel")),
        name="modulate",
    )(x, _per_batch(shift, bsz), _per_batch(scale, bsz))


def _post_norm_kernel(x_ref, y_ref, gate_ref, g_ref, b_ref, sh_ref, sc_ref, xo_ref, *h_ref):
    v = DN_ALPHA * x_ref[0] + gate_ref[0] * y_ref[0].astype(F32)
    cen = v - jnp.mean(v, axis=1, keepdims=True)
    out = cen * lax.rsqrt(jnp.mean(cen * cen, axis=1, keepdims=True) + LN_EPS) * g_ref[...] + b_ref[...]
    xo_ref[0] = out
    if h_ref:
        h_ref[0][0] = (out * (1.0 + sc_ref[0]) + sh_ref[0]).astype(BF16)


def post_norm(x, y, gate, g, b, shift=None, scale=None):
    bsz, length, d = x.shape
    with_h = shift is not None
    if not with_h:
        shift = scale = jnp.zeros((1, d), F32)
    tr = _pick(length, ROW_TILE)
    row = pl.BlockSpec((1, tr, d), lambda b, i: (b, i, 0))
    vec = pl.BlockSpec((1, 1, d), lambda b, i: (b, 0, 0))
    par = pl.BlockSpec((1, d), lambda b, i: (0, 0))
    out = pl.pallas_call(
        _post_norm_kernel, grid=(bsz, length // tr),
        in_specs=[row, row, vec, par, par, vec, vec],
        out_specs=[row, row] if with_h else [row],
        out_shape=[jax.ShapeDtypeStruct(x.shape, F32)] + ([jax.ShapeDtypeStruct(x.shape, BF16)] if with_h else []),
        compiler_params=_cparams(("parallel", "parallel")), name="post_norm",
    )(x, y, _per_batch(gate, bsz), g.reshape(1, d), b.reshape(1, d), _per_batch(shift, bsz), _per_batch(scale, bsz))
    return out if with_h else (out[0], None)


def _seq_neighbours(x, prev_ref, next_ref, n_steps):
    i = pl.program_id(2)
    rows = x.shape[0]
    row = lax.broadcasted_iota(jnp.int32, x.shape, 0)
    before = jnp.where(i == 0, 0.0, prev_ref[0, HALO - 1:HALO, :].astype(F32))
    after = jnp.where(i == n_steps - 1, 0.0, next_ref[0, 0:1, :].astype(F32))
    prev = jnp.where(row == 0, before, pltpu.roll(x, 1, 0))
    nxt = jnp.where(row == rows - 1, after, pltpu.roll(x, rows - 1, 0))
    return prev, nxt


def _seq_specs(tr, tc, length):
    per = tr // HALO
    last = length // HALO - 1
    main = pl.BlockSpec((1, tr, tc), lambda b, j, i: (b, i, j))
    prev = pl.BlockSpec((1, HALO, tc), lambda b, j, i: (b, jnp.maximum(i * per - 1, 0), j))
    nxt = pl.BlockSpec((1, HALO, tc), lambda b, j, i: (b, jnp.minimum((i + 1) * per, last), j))
    return main, prev, nxt


def _hy_short_conv_kernel(x_ref, p_ref, n_ref, w_ref, b_ref, o_ref, *, n_steps):
    x = x_ref[0].astype(F32)
    prev, nxt = _seq_neighbours(x, p_ref, n_ref, n_steps)
    o_ref[0] = (prev * w_ref[0:1, :] + x * w_ref[1:2, :] + nxt * w_ref[2:3, :] + b_ref[...]).astype(o_ref.dtype)


def hy_short_conv(x, w, b):
    bsz, length, ch = x.shape
    tr, tc = _pick(length, 2 * ROW_TILE), _pick(ch, 1024)
    main, prev, nxt = _seq_specs(tr, tc, length)
    return pl.pallas_call(
        functools.partial(_hy_short_conv_kernel, n_steps=length // tr),
        grid=(bsz, ch // tc, length // tr),
        in_specs=[main, prev, nxt, pl.BlockSpec((3, tc), lambda b_, j, i: (0, j)),
                  pl.BlockSpec((1, tc), lambda b_, j, i: (0, j))],
        out_specs=main, out_shape=jax.ShapeDtypeStruct(x.shape, BF16),
        compiler_params=_cparams(("parallel", "parallel", "parallel")), name="hy_short_conv",
    )(x, x, x, w, b.reshape(1, ch))


def _gdn_pre_kernel(x_ref, p_ref, n_ref, w_ref, o_ref, *, n_steps, q_blocks, qk_blocks, q_scale):
    x = x_ref[0].astype(F32)
    prev, nxt = _seq_neighbours(x, p_ref, n_ref, n_steps)
    y = prev * w_ref[0:1, :] + x * w_ref[1:2, :] + nxt * w_ref[2:3, :]
    y = y * jax.nn.sigmoid(y)
    j = pl.program_id(1)
    scale = jnp.where(j < q_blocks, q_scale, 1.0)
    for h in range(y.shape[1] // LANES):
        lanes = slice(h * LANES, (h + 1) * LANES)
        yh = y[:, lanes]
        normed = yh * (lax.rsqrt(jnp.sum(yh * yh, axis=1, keepdims=True) + L2_EPS) * scale)
        o_ref[0, :, lanes] = jnp.where(j < qk_blocks, normed, yh).astype(o_ref.dtype)


def gdn_pre(proj, w_conv, d, dh):
    bsz, length, _ = proj.shape
    tr, tc = _pick(length, 2 * ROW_TILE), _pick(d, 512)
    main, prev, nxt = _seq_specs(tr, tc, length)
    return pl.pallas_call(
        functools.partial(_gdn_pre_kernel, n_steps=length // tr, q_blocks=d // tc, qk_blocks=2 * d // tc,
                          q_scale=dh ** -0.5),
        grid=(bsz, 3 * d // tc, length // tr),
        in_specs=[main, prev, nxt, pl.BlockSpec((3, tc), lambda b_, j, i: (0, j))],
        out_specs=main, out_shape=jax.ShapeDtypeStruct((bsz, length, 3 * d), BF16),
        compiler_params=_cparams(("parallel", "parallel", "parallel")), name="gdn_pre",
    )(proj, proj, proj, w_conv)


def _gdn_post_kernel(of_ref, ob_ref, z_ref, nw_ref, o_ref):
    o = of_ref[0] + ob_ref[0]
    z = z_ref[0].astype(F32)
    gate = z * jax.nn.sigmoid(z) * nw_ref[...]
    for h in range(o.shape[1] // LANES):
        lanes = slice(h * LANES, (h + 1) * LANES)
        oh = o[:, lanes]
        rms = lax.rsqrt(jnp.mean(oh * oh, axis=1, keepdims=True) + RMS_EPS)
        o_ref[0, :, lanes] = (oh * rms * gate[:, lanes]).astype(o_ref.dtype)


def gdn_post(o_f, o_b, proj, norm_w):
    bsz, length, d = o_f.shape
    tr, tc = _pick(length, 2 * ROW_TILE), _pick(d, 512)
    z_off = (proj.shape[2] - d) // tc
    row = pl.BlockSpec((1, tr, tc), lambda b, j, i: (b, i, j))
    nw = jnp.tile(norm_w, tc // norm_w.shape[0]).reshape(1, tc)
    return pl.pallas_call(
        _gdn_post_kernel, grid=(bsz, d // tc, length // tr),
        in_specs=[row, row, pl.BlockSpec((1, tr, tc), lambda b, j, i: (b, i, z_off + j)),
                  pl.BlockSpec((1, tc), lambda b, j, i: (0, 0))],
        out_specs=row, out_shape=jax.ShapeDtypeStruct(o_f.shape, BF16),
        compiler_params=_cparams(("parallel", "parallel", "parallel")), name="gdn_post",
    )(o_f, o_b, proj, nw)


def _ffn_act_kernel(g_ref, v_ref, up_ref, dn_ref, w_ref, b_ref, o_ref, *, n_steps, gw):
    i = pl.program_id(2)
    above = jnp.where(i == 0, 0.0, up_ref[0].astype(F32))
    below = jnp.where(i == n_steps - 1, 0.0, dn_ref[0].astype(F32))
    x = jnp.concatenate([above, g_ref[0].astype(F32), below], axis=0)
    n = x.shape[0]
    col = lax.rem(lax.broadcasted_iota(jnp.int32, x.shape, 0), gw)
    left = jnp.where(col == 0, 0.0, pltpu.roll(x, 1, 0))
    right = jnp.where(col == gw - 1, 0.0, pltpu.roll(x, n - 1, 0))
    rows = g_ref.shape[1]
    acc = b_ref[...]
    for di in range(3):
        rs = slice(di * gw, di * gw + rows)
        acc = acc + (left[rs] * w_ref[3 * di:3 * di + 1, :] + x[rs] * w_ref[3 * di + 1:3 * di + 2, :]
                     + right[rs] * w_ref[3 * di + 2:3 * di + 3, :])
    gelu = 0.5 * acc * (1.0 + lax.erf(acc * (2.0 ** -0.5)))
    o_ref[0] = (gelu * v_ref[0].astype(F32)).astype(o_ref.dtype)


def ffn_act(up, w_dw, b_dw, gw):
    bsz, length, f2 = up.shape
    f = f2 // 2
    tc = _pick(f, 512)
    tr = gw * max(1, min(length // gw, 2 * ROW_TILE // gw))
    per = tr // gw
    last = length // gw - 1
    main = lambda off: pl.BlockSpec((1, tr, tc), lambda b, j, i: (b, i, off + j))
    halo_up = pl.BlockSpec((1, gw, tc), lambda b, j, i: (b, jnp.maximum(i * per - 1, 0), j))
    halo_dn = pl.BlockSpec((1, gw, tc), lambda b, j, i: (b, jnp.minimum((i + 1) * per, last), j))
    return pl.pallas_call(
        functools.partial(_ffn_act_kernel, n_steps=length // tr, gw=gw),
        grid=(bsz, f // tc, length // tr),
        in_specs=[main(0), main(f // tc), halo_up, halo_dn,
                  pl.BlockSpec((9, tc), lambda b, j, i: (0, j)), pl.BlockSpec((1, tc), lambda b, j, i: (0, j))],
        out_specs=main(0), out_shape=jax.ShapeDtypeStruct((bsz, length, f), BF16),
        compiler_params=_cparams(("parallel", "parallel", "parallel")), name="ffn_act",
    )(up, up, up, up, w_dw.reshape(9, f), b_dw.reshape(1, f))


def _proj(h, w, bias=None, out_dtype=F32, **tiles):
    bsz, length, kd = h.shape
    return matmul(h.reshape(bsz * length, kd).astype(BF16), w, bias, out_dtype, **tiles).reshape(bsz, length, -1)


def _hyena_filter_taps(length, p, d):
    t = jnp.linspace(0.0, 1.0, length, dtype=F32)[:, None]
    bands = (HY_EMB_DIM - 1) // 2
    ang = 2.0 * math.pi * jnp.arange(length, dtype=F32)[:, None] / length
    f = jnp.linspace(1e-4, bands - 1, bands, dtype=F32)[None, :]
    z = jnp.concatenate([t, jnp.cos(f * ang), -jnp.sin(f * ang)], axis=-1)
    freq = p['hy_f_freq']
    h = jnp.sin(freq * (z @ p['hy_f_w1'] + p['hy_f_b1']))
    h = jnp.sin(freq * (h @ p['hy_f_w2'] + p['hy_f_b2']))
    h = jnp.sin(freq * (h @ p['hy_f_w3'] + p['hy_f_b3']))
    w_out = p['hy_f_wout'].reshape(-1, 2 * HY_ORDER, d)
    max_decay = math.log(HY_DECAY_TARGET) / HY_FAST_DECAY_PCT
    min_decay = math.log(HY_DECAY_TARGET) / HY_SLOW_DECAY_PCT
    deltas = jnp.abs(jnp.linspace(min_decay, max_decay, d, dtype=F32))
    h_shift = jnp.concatenate([jnp.zeros_like(h[:1]), h[:0:-1]], axis=0)
    out = []
    for o in range(HY_ORDER):
        hf = matmul(h, w_out[:, 2 * o])
        hb_shift = matmul(h_shift, w_out[:, 2 * o + 1])
        k0 = matmul(h[:8], w_out[:, 2 * o + 1])[:1]
        out.append((hf, hb_shift, k0))
    return out, deltas


def _windowed_filter(hf, hb_shift, k0, deltas):
    length = hf.shape[0]
    m = jnp.arange(length, dtype=F32)[:, None]
    rate = deltas / (length - 1)
    head = hf * jnp.exp(-m * rate)
    return jnp.concatenate([head[:1] + k0, head[1:], hb_shift * jnp.exp((m - length) * rate)], axis=0)


def _hyena_mixer(h, p, w_in, w_out):
    bsz, length, d = h.shape
    u = hy_short_conv(_proj(h, w_in, p['hy_b_in'], out_dtype=BF16), p['hy_w_short'], p['hy_b_short'])
    taps, deltas = _hyena_filter_taps(length, p, d)
    z, z_col = u, HY_ORDER * d
    two_stage = (2 * length) % (16 * SLAB) == 0
    for o in range(HY_ORDER):
        if two_stage:
            z = fftconv_gated(z, u, filter_spectrum(*taps[o], deltas), p['hy_skip'][o], z_col, o * d)
        else:
            z = dense_conv_gated(z, u, _windowed_filter(*taps[o], deltas), p['hy_skip'][o], z_col, o * d)
        z_col = 0
    return _proj(z, w_out, p['hy_b_out'])


def _gdn_mixer(h, p, w_qkvz, w_ab, w_out, s0):
    bsz, length, d = h.shape
    nh = p['gdn_a_log'].shape[1]
    dh = d // nh
    proj = _proj(h, w_qkvz, out_dtype=BF16)
    ab = _proj(h, w_ab).reshape(bsz, length, 2, 2, nh)
    qkv = gdn_pre(proj, p['gdn_w_conv'], d, dh)
    g =-jnp.exp(p['gdn_a_log']) * jax.nn.softplus(ab[:, :, 0] + p['gdn_dt_bias'])
    beta = jax.nn.sigmoid(ab[:, :, 1]).transpose(0, 2, 1, 3)
    nch = length // GDN_CHUNK
    g = g.transpose(0, 2, 1, 3).reshape(bsz, 2, nch, GDN_CHUNK, nh)
    g = jnp.stack([jnp.cumsum(g[:, 0], axis=2),
                   jnp.flip(jnp.cumsum(jnp.flip(g[:, 1], axis=2), axis=2), axis=2)], axis=1)
    gcol = g.reshape(bsz, 2, length, nh)
    grow = g.transpose(0, 1, 4, 2, 3)[:, :, :, :, None, :]
    o_f, o_b, s_out = gdn_scan(qkv, gcol, beta, grow, s0, nh)
    return _proj(gdn_post(o_f, o_b, proj, p['gdn_norm_w']), w_out), s_out


FFN_PAD = 1024


def _ffn_weights(p):
    f = p['ffn_b_dw'].shape[0]
    fp = -(-f // FFN_PAD) * FFN_PAD if f > FFN_PAD else f
    pad = lambda a, axis: jnp.pad(a, [(0, fp - f) if ax == axis else (0, 0) for ax in range(a.ndim)])
    w_up = p['ffn_w_up'].astype(BF16)
    w_up = jnp.concatenate([pad(w_up[:, :f], 1), pad(w_up[:, f:], 1)], axis=1)
    return w_up, pad(p['ffn_w_down'].astype(BF16), 0), pad(p['ffn_w_dw'], 2), pad(p['ffn_b_dw'], 0)


def _conv_glu(h, ffn_w, rows, cols):
    w_up, w_down, w_dw, b_dw = ffn_w
    assert h.shape[1] == rows * cols
    return _proj(ffn_act(_proj(h, w_up, out_dtype=BF16), w_dw, b_dw, cols), w_down)


def kernel(x, c, ctx, c_ctx, l0_w_ada, l0_b_ada, l0_ln1_g, l0_ln1_b, l0_ln2_g, l0_ln2_b, l0_hy_w_in, l0_hy_b_in, l0_hy_w_short, l0_hy_b_short, l0_hy_f_w1, l0_hy_f_b1, l0_hy_f_w2, l0_hy_f_b2, l0_hy_f_w3, l0_hy_f_b3, l0_hy_f_wout, l0_hy_f_freq, l0_hy_skip, l0_hy_w_out, l0_hy_b_out, l0_ffn_w_up, l0_ffn_w_dw, l0_ffn_b_dw, l0_ffn_w_down, l1_w_ada, l1_b_ada, l1_ln1_g, l1_ln1_b, l1_ln2_g, l1_ln2_b, l1_gdn_w_in, l1_gdn_w_conv, l1_gdn_a_log, l1_gdn_dt_bias, l1_gdn_norm_w, l1_gdn_w_out, l1_ffn_w_up, l1_ffn_w_dw, l1_ffn_b_dw, l1_ffn_w_down):
    layers = (
        dict(w_ada=l0_w_ada, b_ada=l0_b_ada, ln1_g=l0_ln1_g, ln1_b=l0_ln1_b, ln2_g=l0_ln2_g, ln2_b=l0_ln2_b,
             hy_w_in=l0_hy_w_in, hy_b_in=l0_hy_b_in, hy_w_short=l0_hy_w_short, hy_b_short=l0_hy_b_short,
             hy_f_w1=l0_hy_f_w1, hy_f_b1=l0_hy_f_b1, hy_f_w2=l0_hy_f_w2, hy_f_b2=l0_hy_f_b2,
             hy_f_w3=l0_hy_f_w3, hy_f_b3=l0_hy_f_b3, hy_f_wout=l0_hy_f_wout, hy_f_freq=l0_hy_f_freq,
             hy_skip=l0_hy_skip, hy_w_out=l0_hy_w_out, hy_b_out=l0_hy_b_out,
             ffn_w_up=l0_ffn_w_up, ffn_w_dw=l0_ffn_w_dw, ffn_b_dw=l0_ffn_b_dw, ffn_w_down=l0_ffn_w_down),
        dict(w_ada=l1_w_ada, b_ada=l1_b_ada, ln1_g=l1_ln1_g, ln1_b=l1_ln1_b, ln2_g=l1_ln2_g, ln2_b=l1_ln2_b,
             gdn_w_in=l1_gdn_w_in, gdn_w_conv=l1_gdn_w_conv, gdn_a_log=l1_gdn_a_log,
             gdn_dt_bias=l1_gdn_dt_bias, gdn_norm_w=l1_gdn_norm_w, gdn_w_out=l1_gdn_w_out,
             ffn_w_up=l1_ffn_w_up, ffn_w_dw=l1_ffn_w_dw, ffn_b_dw=l1_ffn_b_dw, ffn_w_down=l1_ffn_w_down),
    )
    bsz, seq, d = x.shape
    rows = seq // GRID_W
    ctx_len = ctx.shape[1]
    cond = jnp.concatenate([c, c_ctx[None], jnp.zeros((8 - bsz - 1, d), F32)], axis=0)
    mods = []
    for p in layers:
        mod = matmul(jax.nn.silu(cond), p['w_ada'], p['b_ada'], tn=512)
        mods.append((jnp.split(mod[:bsz, None, :], 6, axis=-1),
                     jnp.split(mod[bsz:bsz + 1], 6, axis=-1)))
    h_lat = modulate(x, mods[0][0][0], mods[0][0][1])
    h_ctx = modulate(ctx, mods[0][1][0], mods[0][1][1])
    for i, p in enumerate(layers):
        last = i == DEPTH - 1
        (sh1, sc1, gt1, sh2, sc2, gt2), (csh1, csc1, cgt1, csh2, csc2, cgt2) = mods[i]
        nxt, cnxt = (None, None) if last else (mods[i + 1][0][:2], mods[i + 1][1][:2])
        if 'hy_w_in' in p:
            w_in, w_out = p['hy_w_in'].astype(BF16), p['hy_w_out'].astype(BF16)
            y_lat = _hyena_mixer(h_lat, p, w_in, w_out)
            y_ctx = None if last else _hyena_mixer(h_ctx, p, w_in, w_out)
        else:
            nh = p['gdn_a_log'].shape[1]
            w_qkvz = p['gdn_w_in'][:, :4 * d].astype(BF16)
            w_ab = p['gdn_w_in'][:, 4 * d:].astype(BF16)
            w_out = p['gdn_w_out'].astype(BF16)
            s0 = jnp.zeros((bsz, 2, nh, d // nh, d // nh), F32)
            y_ctx, s_ctx = _gdn_mixer(h_ctx, p, w_qkvz, w_ab, w_out, s0)
            y_lat, _ = _gdn_mixer(h_lat, p, w_qkvz, w_ab, w_out, s_ctx)
        ffn_w = _ffn_weights(p)
        x, h_mid = post_norm(x, y_lat, gt1, p['ln1_g'], p['ln1_b'], sh2, sc2)
        ffn = _conv_glu(h_mid, ffn_w, rows, GRID_W)
        x, h_lat = post_norm(x, ffn, gt2, p['ln2_g'], p['ln2_b'], *(nxt or ()))
        if not last:
            ctx, h_mid = post_norm(ctx, y_ctx, cgt1, p['ln1_g'], p['ln1_b'], csh2, csc2)
            ffn = _conv_glu(h_mid, ffn_w, 1, ctx_len)
            ctx, h_ctx = post_norm(ctx, ffn, cgt2, p['ln2_g'], p['ln2_b'], *cnxt)
    return x
```

```python
import functools
import math

import numpy as np
import jax
import jax.numpy as jnp
from jax import lax
from jax.experimental import pallas as pl
from jax.experimental.pallas import tpu as pltpu

F32 = jnp.float32
BF16 = jnp.bfloat16

GRID_W = 64
HY_ORDER = 2
HY_EMB_DIM = 33
HY_DECAY_TARGET = 1e-2
HY_FAST_DECAY_PCT = 0.3
HY_SLOW_DECAY_PCT = 1.5
GDN_CHUNK = 64
LN_EPS = 1e-5
RMS_EPS = 1e-6
L2_EPS = 1e-6
DEPTH = 2
DN_ALPHA = (2 * DEPTH) ** 0.25

LANES = 128
SLAB = 128
SLAB_PITCH = SLAB + 8
VMEM_LIMIT = 56 * 1024 * 1024


def _cparams(sem):
    return pltpu.CompilerParams(dimension_semantics=sem, vmem_limit_bytes=VMEM_LIMIT)


def _mm_kernel(a_ref, w_ref, b_ref, o_ref, acc_ref, *, nk):
    part = jnp.dot(a_ref[...].astype(BF16), w_ref[...].astype(BF16), preferred_element_type=F32)
    if nk == 1:
        o_ref[...] = (part + b_ref[...]).astype(o_ref.dtype)
        return
    k = pl.program_id(2)

    @pl.when(k == 0)
    def _():
        acc_ref[...] = part

    @pl.when(k > 0)
    def _():
        acc_ref[...] += part

    @pl.when(k == nk - 1)
    def _():
        o_ref[...] = (acc_ref[...] + b_ref[...]).astype(o_ref.dtype)


def _pick(n, pref):
    if n <= pref:
        return n
    t = (pref // LANES) * LANES
    while t >= LANES:
        if n % t == 0:
            return t
        t -= LANES
    return n


def matmul(a, w, bias=None, out_dtype=F32, tm=1024, tn=1024, tk=4096):
    m, kd = a.shape
    kd2, n = w.shape
    assert kd == kd2
    tm, tn, tk = _pick(m, tm), _pick(n, tn), _pick(kd, tk)
    nk = kd // tk
    if bias is None:
        bias = jnp.zeros((n,), F32)
    bias = bias.reshape(1, n).astype(F32)
    return pl.pallas_call(
        functools.partial(_mm_kernel, nk=nk),
        grid=(m // tm, n // tn, nk),
        in_specs=[pl.BlockSpec((tm, tk), lambda i, j, k: (i, k)),
                  pl.BlockSpec((tk, tn), lambda i, j, k: (k, j)),
                  pl.BlockSpec((1, tn), lambda i, j, k: (0, j))],
        out_specs=pl.BlockSpec((tm, tn), lambda i, j, k: (i, j)),
        out_shape=jax.ShapeDtypeStruct((m, n), out_dtype),
        scratch_shapes=[pltpu.VMEM((tm, tn) if nk > 1 else (8, LANES), F32)],
        compiler_params=_cparams(("parallel", "parallel", "arbitrary")),
        name="matmul",
    )(a, w, bias)


def _real_form(mat):
    return np.block([[mat.real, -mat.imag], [mat.imag, mat.real]])


@functools.lru_cache(maxsize=None)
def _two_stage_consts(n1, n2):
    n = n1 * n2
    h1 = n1 // 2
    a = np.arange(n1)
    b = np.arange(n2)
    ma = np.exp(-2j * np.pi * np.outer(a, a) / n1)
    mb = np.exp(-2j * np.pi * np.outer(b, b) / n2)
    fa_half = _real_form(ma[:, :h1])
    fa_real = np.concatenate([ma.real, ma.imag], axis=0)
    fb = _real_form(mb)
    fbi = _real_form(np.conj(mb))
    fai = _real_form(np.conj(ma)[:h1, :] / n)
    tw = np.exp(-2j * np.pi * b / n)
    tw = np.stack([np.broadcast_to(tw.real[:, None], (n2, LANES)),
                   np.broadcast_to(tw.imag[:, None], (n2, LANES))])
    return dict(fa_half=fa_half, fa_real=fa_real, fb=fb, fbi=fbi, fai=fai, tw=tw)


@functools.lru_cache(maxsize=None)
def _dense_consts(length):
    n = 2 * length
    f = np.arange(n)
    t = np.arange(length)
    fwd = np.exp(-2j * np.pi * np.outer(f, t) / n)
    fwd_full = np.exp(-2j * np.pi * np.outer(f, f) / n)
    inv = np.exp(2j * np.pi * np.outer(t, f) / n) / n
    return dict(fwd=_real_form(fwd),
                fwd_real=np.concatenate([fwd_full.real, fwd_full.imag], axis=0),
                inv=_real_form(inv))


def _cmul(ar, ai, br, bi):
    return ar * br - ai * bi, ar * bi + ai * br


def _twiddle_init(t_ref):
    t_ref[0] = jnp.ones(t_ref.shape[1:], F32)
    t_ref[1] = jnp.zeros(t_ref.shape[1:], F32)


def _twiddle_step(t_ref, tw_ref, tr, ti):
    nr, ni = _cmul(tr, ti, tw_ref[0], tw_ref[1])
    t_ref[0] = nr
    t_ref[1] = ni


def _filter_spectrum_kernel(feat_ref, featb_ref, wf_ref, wb_ref, delta_ref, fa_ref, fb_ref, tw_ref, o_ref,
                            bufr, bufi, t_ref, *, n1, sb):
    j = pl.program_id(1)
    pitch = SLAB_PITCH
    h1 = n1 // 2
    length = h1 * SLAB

    @pl.when(j == 0)
    def _():
        rate = delta_ref[...] * (1.0 / (length - 1))
        pos = lax.broadcasted_iota(jnp.int32, (SLAB, LANES), 0).astype(F32)
        wf = wf_ref[...].astype(BF16)
        wb = wb_ref[...].astype(BF16)

        def load(a, c):
            src = pl.multiple_of(a * SLAB, SLAB)
            m = pos + lax.convert_element_type(a * SLAB, F32)
            hf = jnp.dot(feat_ref[pl.ds(src, SLAB), :].astype(BF16), wf, preferred_element_type=F32)
            hb = jnp.dot(featb_ref[pl.ds(src, SLAB), :].astype(BF16), wb, preferred_element_type=F32)
            bufr[pl.ds(pl.multiple_of(a * pitch, 8), SLAB), :] = hf * jnp.exp(-m * rate)
            bufr[pl.ds(pl.multiple_of((h1 + a) * pitch, 8), SLAB), :] = hb * jnp.exp((m - length) * rate)
            return c
        lax.fori_loop(0, h1, load, 0)
        tap0 = jnp.dot(feat_ref[0:8, :].astype(BF16), wb, preferred_element_type=F32)
        bufr[0:1, :] = bufr[0:1, :] + tap0[0:1]
        fa = fa_ref[...]

        def stage_a(b, c):
            x = bufr[pl.ds(b, n1, stride=pitch), :].astype(BF16)
            y = jnp.dot(fa, x, preferred_element_type=F32)
            bufr[pl.ds(b, n1, stride=pitch), :] = y[:n1]
            bufi[pl.ds(b, n1, stride=pitch), :] = y[n1:]
            return c
        lax.fori_loop(0, SLAB, stage_a, 0, unroll=4)
        _twiddle_init(t_ref)

    fb = fb_ref[...]

    def stage_b(s, c):
        k1 = j * sb + s
        row = pl.multiple_of(k1 * pitch, 8)
        tr, ti = t_ref[0], t_ref[1]
        yr, yi = _cmul(bufr[pl.ds(row, SLAB), :], bufi[pl.ds(row, SLAB), :], tr, ti)
        x = jnp.concatenate([yr, yi], axis=0).astype(BF16)
        spec = jnp.dot(fb, x, preferred_element_type=F32)
        out = pl.multiple_of(s * SLAB, SLAB)
        o_ref[0, pl.ds(out, SLAB), :] = spec[:SLAB]
        o_ref[1, pl.ds(out, SLAB), :] = spec[SLAB:]
        _twiddle_step(t_ref, tw_ref, tr, ti)
        return c
    lax.fori_loop(0, sb, stage_b, 0, unroll=4)


def _slab_blocks(n1):
    return 8 if n1 % 8 == 0 and n1 >= 16 else 1


def filter_spectrum(feat, feat_circ, w_out, fwd_group, bwd_group, deltas):
    length, fh = feat.shape
    d = deltas.shape[0]
    n = 2 * length
    n1 = n // SLAB
    kb = _slab_blocks(n1)
    sb = n1 // kb
    c = _two_stage_consts(n1, SLAB)
    fa = jnp.asarray(c["fa_real"], BF16)
    fb = jnp.asarray(c["fb"], BF16)
    tw = jnp.asarray(c["tw"], F32)
    const = lambda shape: pl.BlockSpec(shape, lambda i, j: (0,) * len(shape))
    return pl.pallas_call(
        functools.partial(_filter_spectrum_kernel, n1=n1, sb=sb),
        grid=(d // LANES, kb),
        in_specs=[const(feat.shape), const(feat.shape),
                  pl.BlockSpec((fh, LANES), lambda i, j: (0, fwd_group * (d // LANES) + i)),
                  pl.BlockSpec((fh, LANES), lambda i, j: (0, bwd_group * (d // LANES) + i)),
                  pl.BlockSpec((1, LANES), lambda i, j: (0, i)),
                  const(fa.shape), const(fb.shape), const(tw.shape)],
        out_specs=pl.BlockSpec((2, sb * SLAB, LANES), lambda i, j: (0, j, i)),
        out_shape=jax.ShapeDtypeStruct((2, n, d), F32),
        scratch_shapes=[pltpu.VMEM((n1 * SLAB_PITCH, LANES), F32),
                        pltpu.VMEM((n1 * SLAB_PITCH, LANES), F32),
                        pltpu.VMEM((2, SLAB, LANES), F32)],
        compiler_params=_cparams(("parallel", "arbitrary")),
        name="filter_spectrum",
    )(feat, feat_circ, w_out, w_out, deltas.reshape(1, d), fa, fb, tw)


def _fftconv_kernel(z_ref, g_ref, ks_ref, skip_ref, fa_ref, fb_ref, fbi_ref, fai_ref, tw_ref, o_ref,
                    bufr, bufi, t_ref, *, n1, sb, kb):
    j = pl.program_id(1)
    pitch = SLAB_PITCH
    h1 = n1 // 2

    @pl.when(j == 0)
    def _():
        def load(a, c):
            src = pl.multiple_of(a * SLAB, SLAB)
            dst = pl.multiple_of(a * pitch, 8)
            bufr[pl.ds(dst, SLAB), :] = z_ref[0, pl.ds(src, SLAB), :].astype(F32)
            bufi[pl.ds(dst, SLAB), :] = z_ref[1, pl.ds(src, SLAB), :].astype(F32)
            return c
        lax.fori_loop(0, h1, load, 0)
        fa = fa_ref[...]

        def stage_a(b, c):
            x = jnp.concatenate([bufr[pl.ds(b, h1, stride=pitch), :],
                                 bufi[pl.ds(b, h1, stride=pitch), :]], axis=0).astype(BF16)
            y = jnp.dot(fa, x, preferred_element_type=F32)
            bufr[pl.ds(b, n1, stride=pitch), :] = y[:n1]
            bufi[pl.ds(b, n1, stride=pitch), :] = y[n1:]
            return c
        lax.fori_loop(0, SLAB, stage_a, 0, unroll=4)
        _twiddle_init(t_ref)

    fb = fb_ref[...]
    fbi = fbi_ref[...]

    def stage_b(s, c):
        k1 = j * sb + s
        row = pl.multiple_of(k1 * pitch, 8)
        tr, ti = t_ref[0], t_ref[1]
        yr, yi = _cmul(bufr[pl.ds(row, SLAB), :], bufi[pl.ds(row, SLAB), :], tr, ti)
        spec = jnp.dot(fb, jnp.concatenate([yr, yi], axis=0).astype(BF16), preferred_element_type=F32)
        krow = pl.multiple_of(s * SLAB, SLAB)
        pr, pi = _cmul(spec[:SLAB], spec[SLAB:], ks_ref[0, pl.ds(krow, SLAB), :], ks_ref[1, pl.ds(krow, SLAB), :])
        back = jnp.dot(fbi, jnp.concatenate([pr, pi], axis=0).astype(BF16), preferred_element_type=F32)
        ur, ui = _cmul(back[:SLAB], back[SLAB:], tr, -ti)
        bufr[pl.ds(row, SLAB), :] = ur
        bufi[pl.ds(row, SLAB), :] = ui
        _twiddle_step(t_ref, tw_ref, tr, ti)
        return c
    lax.fori_loop(0, sb, stage_b, 0, unroll=4)

    @pl.when(j == kb - 1)
    def _():
        fai = fai_ref[...]

        def stage_ai(b, c):
            x = jnp.concatenate([bufr[pl.ds(b, n1, stride=pitch), :],
                                 bufi[pl.ds(b, n1, stride=pitch), :]], axis=0).astype(BF16)
            y = jnp.dot(fai, x, preferred_element_type=F32)
            bufr[pl.ds(b, h1, stride=pitch), :] = y[:h1]
            bufi[pl.ds(b, h1, stride=pitch), :] = y[h1:]
            return c
        lax.fori_loop(0, SLAB, stage_ai, 0, unroll=4)
        skip = skip_ref[...]

        def store(a, c):
            dst = pl.multiple_of(a * SLAB, SLAB)
            src = pl.multiple_of(a * pitch, 8)
            for bi, buf in enumerate((bufr, bufi)):
                zz = z_ref[bi, pl.ds(dst, SLAB), :].astype(F32)
                gg = g_ref[bi, pl.ds(dst, SLAB), :].astype(F32)
                o_ref[bi, pl.ds(dst, SLAB), :] = (gg * (buf[pl.ds(src, SLAB), :] + zz * skip)).astype(o_ref.dtype)
            return c
        lax.fori_loop(0, h1, store, 0)


def fftconv_gated(z, gate, kspec, skip, z_col=0, gate_col=0, out_dtype=BF16):
    bsz, length, _ = z.shape
    d = skip.shape[0]
    assert bsz == 2
    n1 = 2 * length // SLAB
    kb = _slab_blocks(n1)
    sb = n1 // kb
    c = _two_stage_consts(n1, SLAB)
    fa = jnp.asarray(c["fa_half"], BF16)
    fb = jnp.asarray(c["fb"], BF16)
    fbi = jnp.asarray(c["fbi"], BF16)
    fai = jnp.asarray(c["fai"], BF16)
    tw = jnp.asarray(c["tw"], F32)
    const = lambda shape: pl.BlockSpec(shape, lambda i, j: (0,) * len(shape))
    seq = lambda col=0: pl.BlockSpec((2, length, LANES), lambda i, j: (0, 0, col // LANES + i))
    return pl.pallas_call(
        functools.partial(_fftconv_kernel, n1=n1, sb=sb, kb=kb),
        grid=(d // LANES, kb),
        in_specs=[seq(z_col), seq(gate_col),
                  pl.BlockSpec((2, sb * SLAB, LANES), lambda i, j: (0, j, i)),
                  pl.BlockSpec((1, LANES), lambda i, j: (0, i)),
                  const(fa.shape), const(fb.shape), const(fbi.shape), const(fai.shape), const(tw.shape)],
        out_specs=seq(),
        out_shape=jax.ShapeDtypeStruct((2, length, d), out_dtype),
        scratch_shapes=[pltpu.VMEM((n1 * SLAB_PITCH, LANES), F32),
                        pltpu.VMEM((n1 * SLAB_PITCH, LANES), F32),
                        pltpu.VMEM((2, SLAB, LANES), F32)],
        compiler_params=_cparams(("parallel", "arbitrary")),
        name="fftconv",
    )(z, gate, kspec, skip.reshape(1, d), fa, fb, fbi, fai, tw)


def _dense_conv_kernel(z_ref, g_ref, k_ref, skip_ref, fwd_ref, fwdk_ref, inv_ref, o_ref, *, length):
    n = 2 * length
    x = jnp.concatenate([z_ref[0], z_ref[1]], axis=0)
    spec = jnp.dot(fwd_ref[...], x.astype(BF16), preferred_element_type=F32)
    kspec = jnp.dot(fwdk_ref[...], k_ref[...].astype(BF16), preferred_element_type=F32)
    pr, pi = _cmul(spec[:n], spec[n:], kspec[:n], kspec[n:])
    y = jnp.dot(inv_ref[...], jnp.concatenate([pr, pi], axis=0).astype(BF16), preferred_element_type=F32)
    skip = skip_ref[...]
    for bi in range(2):
        yy = y[bi * length:(bi + 1) * length]
        o_ref[bi] = (g_ref[bi].astype(F32) * (yy + z_ref[bi].astype(F32) * skip)).astype(o_ref.dtype)


def dense_conv_gated(z, gate, k_time, skip, z_col=0, gate_col=0, out_dtype=BF16):
    bsz, length, _ = z.shape
    d = skip.shape[0]
    assert bsz == 2
    c = _dense_consts(length)
    fwd = jnp.asarray(c["fwd"], BF16)
    fwdk = jnp.asarray(c["fwd_real"], BF16)
    inv = jnp.asarray(c["inv"], BF16)
    tile = 2 * LANES if d % (2 * LANES) == 0 else LANES
    const = lambda shape: pl.BlockSpec(shape, lambda i: (0,) * len(shape))
    seq = lambda col=0: pl.BlockSpec((2, length, tile), lambda i: (0, 0, col // tile + i))
    return pl.pallas_call(
        functools.partial(_dense_conv_kernel, length=length),
        grid=(d // tile,),
        in_specs=[seq(z_col), seq(gate_col), pl.BlockSpec((2 * length, tile), lambda i: (0, i)),
                  pl.BlockSpec((1, tile), lambda i: (0, i)),
                  const(fwd.shape), const(fwdk.shape), const(inv.shape)],
        out_specs=seq(),
        out_shape=jax.ShapeDtypeStruct((2, length, d), out_dtype),
        compiler_params=_cparams(("parallel",)),
        name="dense_conv",
    )(z, gate, k_time, skip.reshape(1, d), fwd, fwdk, inv)


def _bdot(a, b):
    return lax.dot_general(a.astype(BF16), b.astype(BF16), (((2,), (1,)), ((0,), (0,))),
                           preferred_element_type=F32)


def _bdot_nt(a, b):
    return lax.dot_general(a.astype(BF16), b.astype(BF16), (((2,), (2,)), ((0,), (0,))),
                           preferred_element_type=F32)


def _dot(a, b):
    return jnp.dot(a.astype(BF16), b.astype(BF16), preferred_element_type=F32)


def _dot_tn(a, b):
    return lax.dot_general(a.astype(BF16), b.astype(BF16), (((0,), (0,)), ((), ())), preferred_element_type=F32)


def _gdn_kernel(qf_ref, kf_ref, vf_ref, qb_ref, kb_ref, vb_ref, gcf_ref, bcf_ref, gcb_ref, bcb_ref,
                grf_ref, grb_ref, s0_ref, of_ref, ob_ref, sout_ref,
                s_ref, u_ref, w_ref, qg_ref, at_ref, gl_ref, mw_ref, n_ref, sc_ref, *, r, nb, hb, nh):
    cs = GDN_CHUNK
    head0 = pl.program_id(1) * hb
    j = pl.program_id(2)

    @pl.when(j == 0)
    def _():
        s_ref[...] = s0_ref[0]

    ii = lax.broadcasted_iota(jnp.int32, (cs, cs), 0)
    jj = lax.broadcasted_iota(jnp.int32, (cs, cs), 1)
    eye = (ii == jj).astype(F32)
    blk_same = [jnp.right_shift(ii, lvl) == jnp.right_shift(jj, lvl) for lvl in range(int(math.log2(cs)) + 1)]
    lane_head = lax.broadcasted_iota(jnp.int32, (r * cs, nh), 1)

    for dr in range(2):
        q_ref, k_ref, v_ref = (qf_ref, kf_ref, vf_ref) if dr == 0 else (qb_ref, kb_ref, vb_ref)
        gcol_ref, bcol_ref = (gcf_ref, bcf_ref) if dr == 0 else (gcb_ref, bcb_ref)
        grow_ref = grf_ref if dr == 0 else grb_ref
        incl = (ii >= jj) if dr == 0 else (ii <= jj)
        strict = (ii > jj) if dr == 0 else (ii < jj)
        last = cs - 1 if dr == 0 else 0
        def heads(ref):
            return jnp.concatenate([ref[0, :, hh * LANES:(hh + 1) * LANES].astype(F32).reshape(r, cs, LANES)
                                    for hh in range(hb)], axis=0)

        def head_cols(ref):
            return jnp.concatenate(
                [jnp.sum(jnp.where(lane_head == head0 + hh, ref[0, 0], 0.0), axis=1, keepdims=True).reshape(r, cs, 1)
                 for hh in range(hb)], axis=0)
        q, k, v = heads(q_ref), heads(k_ref), heads(v_ref)
        gc, bc = head_cols(gcol_ref), head_cols(bcol_ref)
        gr = grow_ref[0, 0].reshape(hb * r, 1, cs)
        g_last = gr[:, :, last:last + 1]
        decay = jnp.where(incl, jnp.exp(jnp.where(incl, gc - gr, 0.0)), 0.0)
        kbeta = k * bc
        a_mat = jnp.where(strict, _bdot_nt(kbeta, k) * decay, 0.0).astype(BF16).astype(F32)
        t = eye - jnp.where(blk_same[1] & ~blk_same[0], a_mat, 0.0)
        for lvl in range(1, len(blk_same) - 1):
            x = jnp.where(blk_same[lvl + 1] & ~blk_same[lvl], a_mat, 0.0)
            t = t - _bdot(t, _bdot(x, t))
        t_hi = t.astype(BF16)
        a_t = _bdot(a_mat, t_hi) + _bdot(a_mat, t - t_hi.astype(F32))
        t = t + _bdot(t, eye - t - a_t)
        eg = jnp.exp(gc)
        sol = _bdot(t, jnp.concatenate([v * bc, kbeta * eg], axis=2))
        u, w = sol[:, :, :LANES], sol[:, :, LANES:]
        per_head = lambda a: a.reshape((hb, r) + a.shape[1:])
        u_ref[dr] = per_head(u)
        w_ref[dr] = per_head(w)
        at_ref[dr] = per_head(jnp.where(incl, _bdot_nt(q, k) * decay, 0.0))
        qg_ref[dr] = per_head(q * eg)
        kg_t = jnp.swapaxes(k * jnp.exp(g_last - gc), 1, 2)
        mw_ref[dr] = per_head(-_bdot(kg_t, w))
        n_ref[dr] = per_head(_bdot(kg_t, u))
        gl_ref[dr] = per_head(jnp.broadcast_to(jnp.exp(g_last), (hb * r, 8, LANES)))

    def scan(p, carry):
        for dr in range(2):
            c = p if dr == 0 else r - 1 - p
            for hh in range(hb):
                s = s_ref[dr, hh]
                sc_ref[dr, hh, c] = s
                s_ref[dr, hh] = s * gl_ref[dr, hh, c][0:1, :] + _dot(mw_ref[dr, hh, c], s) + n_ref[dr, hh, c]
        return carry
    lax.fori_loop(0, r, scan, 0, unroll=2)

    for dr in range(2):
        o_ref = of_ref if dr == 0 else ob_ref
        flat = lambda ref: ref[dr].reshape((hb * r,) + ref.shape[3:])
        sc = flat(sc_ref)
        v_new = flat(u_ref) - _bdot(flat(w_ref), sc)
        o = _bdot(flat(qg_ref), sc) + _bdot(flat(at_ref), v_new)
        for hh in range(hb):
            o_ref[0, :, hh * LANES:(hh + 1) * LANES] = (
                o[hh * r:(hh + 1) * r].reshape(r * cs, LANES).astype(o_ref.dtype))

    @pl.when(j == nb - 1)
    def _():
        sout_ref[0] = s_ref[...]


def gdn_scan(qkv, gcol, bcol, grow, s0, nh):
    bsz, length, d3 = qkv.shape
    d = d3 // 3
    dh = d // nh
    assert dh == LANES
    hb = 4 if nh % 4 == 0 else 2 if nh % 2 == 0 else 1
    ng = nh // hb
    nch = length // GDN_CHUNK
    r = 8 if nch % 8 == 0 else nch
    nb = nch // r
    rows = r * GDN_CHUNK
    blk = lambda j, rev: nb - 1 - j if rev else j
    seq = lambda off, rev: pl.BlockSpec((1, rows, hb * LANES), lambda b, g, j: (b, blk(j, rev), off * ng + g))
    col = lambda rev: pl.BlockSpec((1, 1, rows, nh), lambda b, g, j: (b, rev, blk(j, rev), 0))
    row = lambda rev: pl.BlockSpec((1, 1, hb, r, 1, GDN_CHUNK), lambda b, g, j: (b, rev, g, blk(j, rev), 0, 0))
    out = lambda rev: pl.BlockSpec((1, rows, hb * LANES), lambda b, g, j: (b, blk(j, rev), g))
    state = pl.BlockSpec((1, 2, hb, dh, dh), lambda b, g, j: (b, 0, g, 0, 0))
    per_chunk = lambda *shape: pltpu.VMEM((2, hb, r) + shape, F32)
    return pl.pallas_call(
        functools.partial(_gdn_kernel, r=r, nb=nb, hb=hb, nh=nh),
        grid=(bsz, ng, nb),
        in_specs=[seq(0, 0), seq(1, 0), seq(2, 0), seq(0, 1), seq(1, 1), seq(2, 1),
                  col(0), col(0), col(1), col(1), row(0), row(1), state],
        out_specs=[out(0), out(1), state],
        out_shape=[jax.ShapeDtypeStruct((bsz, length, d), BF16),
                   jax.ShapeDtypeStruct((bsz, length, d), BF16),
                   jax.ShapeDtypeStruct((bsz, 2, nh, dh, dh), F32)],
        scratch_shapes=[pltpu.VMEM((2, hb, dh, dh), F32),
                        per_chunk(GDN_CHUNK, LANES), per_chunk(GDN_CHUNK, LANES), per_chunk(GDN_CHUNK, LANES),
                        per_chunk(GDN_CHUNK, GDN_CHUNK), per_chunk(8, LANES),
                        per_chunk(dh, dh), per_chunk(dh, dh), per_chunk(dh, dh)],
        compiler_params=_cparams(("parallel", "parallel", "arbitrary")),
        name="gdn_scan",
    )(qkv, qkv, qkv, qkv, qkv, qkv, gcol, bcol, gcol, bcol, grow, grow, s0)


ROW_TILE = 256
HALO = 16


def _per_batch(v, bsz):
    return jnp.broadcast_to(v.reshape(-1, 1, v.shape[-1]), (bsz, 1, v.shape[-1]))


def _modulate_kernel(x_ref, sh_ref, sc_ref, h_ref):
    h_ref[0] = (x_ref[0] * (1.0 + sc_ref[0]) + sh_ref[0]).astype(h_ref.dtype)


def modulate(x, shift, scale):
    bsz, length, d = x.shape
    tr = _pick(length, ROW_TILE)
    row = pl.BlockSpec((1, tr, d), lambda b, i: (b, i, 0))
    vec = pl.BlockSpec((1, 1, d), lambda b, i: (b, 0, 0))
    return pl.pallas_call(
        _modulate_kernel, grid=(bsz, length // tr), in_specs=[row, vec, vec], out_specs=row,
        out_shape=jax.ShapeDtypeStruct(x.shape, BF16), compiler_params=_cparams(("parallel", "parallel")),
        name="modulate",
    )(x, _per_batch(shift, bsz), _per_batch(scale, bsz))


def _post_norm_kernel(x_ref, y_ref, gate_ref, g_ref, b_ref, sh_ref, sc_ref, xo_ref, *h_ref):
    v = DN_ALPHA * x_ref[0] + gate_ref[0] * y_ref[0].astype(F32)
    cen = v - jnp.mean(v, axis=1, keepdims=True)
    out = cen * lax.rsqrt(jnp.mean(cen * cen, axis=1, keepdims=True) + LN_EPS) * g_ref[...] + b_ref[...]
    xo_ref[0] = out
    if h_ref:
        h_ref[0][0] = (out * (1.0 + sc_ref[0]) + sh_ref[0]).astype(BF16)


def post_norm(x, y, gate, g, b, shift=None, scale=None):
    bsz, length, d = x.shape
    with_h = shift is not None
    if not with_h:
        shift = scale = jnp.zeros((1, d), F32)
    tr = _pick(length, ROW_TILE)
    row = pl.BlockSpec((1, tr, d), lambda b, i: (b, i, 0))
    vec = pl.BlockSpec((1, 1, d), lambda b, i: (b, 0, 0))
    par = pl.BlockSpec((1, d), lambda b, i: (0, 0))
    out = pl.pallas_call(
        _post_norm_kernel, grid=(bsz, length // tr),
        in_specs=[row, row, vec, par, par, vec, vec],
        out_specs=[row, row] if with_h else [row],
        out_shape=[jax.ShapeDtypeStruct(x.shape, F32)] + ([jax.ShapeDtypeStruct(x.shape, BF16)] if with_h else []),
        compiler_params=_cparams(("parallel", "parallel")), name="post_norm",
    )(x, y, _per_batch(gate, bsz), g.reshape(1, d), b.reshape(1, d), _per_batch(shift, bsz), _per_batch(scale, bsz))
    return out if with_h else (out[0], None)


def _seq_neighbours(x, prev_ref, next_ref, n_steps):
    i = pl.program_id(2)
    rows = x.shape[0]
    row = lax.broadcasted_iota(jnp.int32, x.shape, 0)
    before = jnp.where(i == 0, 0.0, prev_ref[0, HALO - 1:HALO, :].astype(F32))
    after = jnp.where(i == n_steps - 1, 0.0, next_ref[0, 0:1, :].astype(F32))
    prev = jnp.where(row == 0, before, pltpu.roll(x, 1, 0))
    nxt = jnp.where(row == rows - 1, after, pltpu.roll(x, rows - 1, 0))
    return prev, nxt


def _seq_specs(tr, tc, length):
    per = tr // HALO
    last = length // HALO - 1
    main = pl.BlockSpec((1, tr, tc), lambda b, j, i: (b, i, j))
    prev = pl.BlockSpec((1, HALO, tc), lambda b, j, i: (b, jnp.maximum(i * per - 1, 0), j))
    nxt = pl.BlockSpec((1, HALO, tc), lambda b, j, i: (b, jnp.minimum((i + 1) * per, last), j))
    return main, prev, nxt


def _hy_short_conv_kernel(x_ref, p_ref, n_ref, w_ref, b_ref, o_ref, *, n_steps):
    x = x_ref[0].astype(F32)
    prev, nxt = _seq_neighbours(x, p_ref, n_ref, n_steps)
    o_ref[0] = (prev * w_ref[0:1, :] + x * w_ref[1:2, :] + nxt * w_ref[2:3, :] + b_ref[...]).astype(o_ref.dtype)


def hy_short_conv(x, w, b):
    bsz, length, ch = x.shape
    tr, tc = _pick(length, 2 * ROW_TILE), _pick(ch, 1024)
    main, prev, nxt = _seq_specs(tr, tc, length)
    return pl.pallas_call(
        functools.partial(_hy_short_conv_kernel, n_steps=length // tr),
        grid=(bsz, ch // tc, length // tr),
        in_specs=[main, prev, nxt, pl.BlockSpec((3, tc), lambda b_, j, i: (0, j)),
                  pl.BlockSpec((1, tc), lambda b_, j, i: (0, j))],
        out_specs=main, out_shape=jax.ShapeDtypeStruct(x.shape, BF16),
        compiler_params=_cparams(("parallel", "parallel", "parallel")), name="hy_short_conv",
    )(x, x, x, w, b.reshape(1, ch))


def _gdn_pre_kernel(x_ref, p_ref, n_ref, w_ref, o_ref, *, n_steps, q_blocks, qk_blocks, q_scale):
    x = x_ref[0].astype(F32)
    prev, nxt = _seq_neighbours(x, p_ref, n_ref, n_steps)
    y = prev * w_ref[0:1, :] + x * w_ref[1:2, :] + nxt * w_ref[2:3, :]
    y = y * jax.nn.sigmoid(y)
    j = pl.program_id(1)
    scale = jnp.where(j < q_blocks, q_scale, 1.0)
    for h in range(y.shape[1] // LANES):
        lanes = slice(h * LANES, (h + 1) * LANES)
        yh = y[:, lanes]
        normed = yh * (lax.rsqrt(jnp.sum(yh * yh, axis=1, keepdims=True) + L2_EPS) * scale)
        o_ref[0, :, lanes] = jnp.where(j < qk_blocks, normed, yh).astype(o_ref.dtype)


def gdn_pre(proj, w_conv, d, dh):
    bsz, length, _ = proj.shape
    tr, tc = _pick(length, 2 * ROW_TILE), _pick(d, 1024)
    main, prev, nxt = _seq_specs(tr, tc, length)
    return pl.pallas_call(
        functools.partial(_gdn_pre_kernel, n_steps=length // tr, q_blocks=d // tc, qk_blocks=2 * d // tc,
                          q_scale=dh ** -0.5),
        grid=(bsz, 3 * d // tc, length // tr),
        in_specs=[main, prev, nxt, pl.BlockSpec((3, tc), lambda b_, j, i: (0, j))],
        out_specs=main, out_shape=jax.ShapeDtypeStruct((bsz, length, 3 * d), BF16),
        compiler_params=_cparams(("parallel", "parallel", "parallel")), name="gdn_pre",
    )(proj, proj, proj, w_conv)


def _gdn_post_kernel(of_ref, ob_ref, z_ref, nw_ref, o_ref):
    o = of_ref[0].astype(F32) + ob_ref[0].astype(F32)
    z = z_ref[0].astype(F32)
    gate = z * jax.nn.sigmoid(z) * nw_ref[...]
    for h in range(o.shape[1] // LANES):
        lanes = slice(h * LANES, (h + 1) * LANES)
        oh = o[:, lanes]
        rms = lax.rsqrt(jnp.mean(oh * oh, axis=1, keepdims=True) + RMS_EPS)
        o_ref[0, :, lanes] = (oh * rms * gate[:, lanes]).astype(o_ref.dtype)


def gdn_post(o_f, o_b, proj, norm_w):
    bsz, length, d = o_f.shape
    tr, tc = _pick(length, 2 * ROW_TILE), _pick(d, 512)
    z_off = (proj.shape[2] - d) // tc
    row = pl.BlockSpec((1, tr, tc), lambda b, j, i: (b, i, j))
    nw = jnp.tile(norm_w, tc // norm_w.shape[0]).reshape(1, tc)
    return pl.pallas_call(
        _gdn_post_kernel, grid=(bsz, d // tc, length // tr),
        in_specs=[row, row, pl.BlockSpec((1, tr, tc), lambda b, j, i: (b, i, z_off + j)),
                  pl.BlockSpec((1, tc), lambda b, j, i: (0, 0))],
        out_specs=row, out_shape=jax.ShapeDtypeStruct(o_f.shape, BF16),
        compiler_params=_cparams(("parallel", "parallel", "parallel")), name="gdn_post",
    )(o_f, o_b, proj, nw)


def _ffn_act_kernel(g_ref, v_ref, up_ref, dn_ref, w_ref, b_ref, o_ref, *, n_steps, gw):
    i = pl.program_id(2)
    above = jnp.where(i == 0, 0.0, up_ref[0].astype(F32))
    below = jnp.where(i == n_steps - 1, 0.0, dn_ref[0].astype(F32))
    x = jnp.concatenate([above, g_ref[0].astype(F32), below], axis=0)
    n = x.shape[0]
    col = lax.rem(lax.broadcasted_iota(jnp.int32, x.shape, 0), gw)
    left = jnp.where(col == 0, 0.0, pltpu.roll(x, 1, 0))
    right = jnp.where(col == gw - 1, 0.0, pltpu.roll(x, n - 1, 0))
    rows = g_ref.shape[1]
    acc = b_ref[...]
    for di in range(3):
        rs = slice(di * gw, di * gw + rows)
        acc = acc + (left[rs] * w_ref[3 * di:3 * di + 1, :] + x[rs] * w_ref[3 * di + 1:3 * di + 2, :]
                     + right[rs] * w_ref[3 * di + 2:3 * di + 3, :])
    gelu = 0.5 * acc * (1.0 + lax.erf(acc * (2.0 ** -0.5)))
    o_ref[0] = (gelu * v_ref[0].astype(F32)).astype(o_ref.dtype)


def ffn_act(up, w_dw, b_dw, gw):
    bsz, length, f2 = up.shape
    f = f2 // 2
    tc = _pick(f, 1024)
    tr = gw * max(1, min(length // gw, 2 * ROW_TILE // gw))
    per = tr // gw
    last = length // gw - 1
    main = lambda off: pl.BlockSpec((1, tr, tc), lambda b, j, i: (b, i, off + j))
    halo_up = pl.BlockSpec((1, gw, tc), lambda b, j, i: (b, jnp.maximum(i * per - 1, 0), j))
    halo_dn = pl.BlockSpec((1, gw, tc), lambda b, j, i: (b, jnp.minimum((i + 1) * per, last), j))
    return pl.pallas_call(
        functools.partial(_ffn_act_kernel, n_steps=length // tr, gw=gw),
        grid=(bsz, f // tc, length // tr),
        in_specs=[main(0), main(f // tc), halo_up, halo_dn,
                  pl.BlockSpec((9, tc), lambda b, j, i: (0, j)), pl.BlockSpec((1, tc), lambda b, j, i: (0, j))],
        out_specs=main(0), out_shape=jax.ShapeDtypeStruct((bsz, length, f), BF16),
        compiler_params=_cparams(("parallel", "parallel", "parallel")), name="ffn_act",
    )(up, up, up, up, w_dw.reshape(9, f), b_dw.reshape(1, f))


def _proj(h, w, bias=None, out_dtype=F32, **tiles):
    bsz, length, kd = h.shape
    return matmul(h.reshape(bsz * length, kd).astype(BF16), w, bias, out_dtype, **tiles).reshape(bsz, length, -1)


def _hyena_filter_features(length, p, d):
    t = jnp.linspace(0.0, 1.0, length, dtype=F32)[:, None]
    bands = (HY_EMB_DIM - 1) // 2
    ang = 2.0 * math.pi * jnp.arange(length, dtype=F32)[:, None] / length
    f = jnp.linspace(1e-4, bands - 1, bands, dtype=F32)[None, :]
    z = jnp.concatenate([t, jnp.cos(f * ang), -jnp.sin(f * ang)], axis=-1)
    freq = p['hy_f_freq']
    h = jnp.sin(freq * (z @ p['hy_f_w1'] + p['hy_f_b1']))
    h = jnp.sin(freq * (h @ p['hy_f_w2'] + p['hy_f_b2']))
    h = jnp.sin(freq * (h @ p['hy_f_w3'] + p['hy_f_b3']))
    max_decay = math.log(HY_DECAY_TARGET) / HY_FAST_DECAY_PCT
    min_decay = math.log(HY_DECAY_TARGET) / HY_SLOW_DECAY_PCT
    deltas = jnp.abs(jnp.linspace(min_decay, max_decay, d, dtype=F32))
    return h, jnp.concatenate([jnp.zeros_like(h[:1]), h[:0:-1]], axis=0), deltas


def _windowed_filter(feat, feat_circ, w_fwd, w_bwd, deltas):
    length = feat.shape[0]
    m = jnp.arange(length, dtype=F32)[:, None]
    rate = deltas / (length - 1)
    head = matmul(feat, w_fwd) * jnp.exp(-m * rate)
    tail = matmul(feat_circ, w_bwd) * jnp.exp((m - length) * rate)
    return jnp.concatenate([head[:1] + matmul(feat[:8], w_bwd)[:1], head[1:], tail], axis=0)


def _hyena_mixer(h, p, w_in, w_out):
    bsz, length, d = h.shape
    u = hy_short_conv(_proj(h, w_in, p['hy_b_in'], out_dtype=BF16), p['hy_w_short'], p['hy_b_short'])
    feat, feat_circ, deltas = _hyena_filter_features(length, p, d)
    w_filt = p['hy_f_wout']
    z, z_col = u, HY_ORDER * d
    two_stage = (2 * length) % (16 * SLAB) == 0
    for o in range(HY_ORDER):
        if two_stage:
            kspec = filter_spectrum(feat, feat_circ, w_filt, 2 * o, 2 * o + 1, deltas)
            z = fftconv_gated(z, u, kspec, p['hy_skip'][o], z_col, o * d)
        else:
            k_time = _windowed_filter(feat, feat_circ, w_filt[:, 2 * o * d:(2 * o + 1) * d],
                                      w_filt[:, (2 * o + 1) * d:(2 * o + 2) * d], deltas)
            z = dense_conv_gated(z, u, k_time, p['hy_skip'][o], z_col, o * d)
        z_col = 0
    return _proj(z, w_out, p['hy_b_out'], out_dtype=BF16)


def _gdn_mixer(h, p, w_qkvz, w_ab, w_out, s0):
    bsz, length, d = h.shape
    nh = p['gdn_a_log'].shape[1]
    dh = d // nh
    proj = _proj(h, w_qkvz, out_dtype=BF16)
    ab = _proj(h, w_ab).reshape(bsz, length, 2, 2, nh)
    qkv = gdn_pre(proj, p['gdn_w_conv'], d, dh)
    g =-jnp.exp(p['gdn_a_log']) * jax.nn.softplus(ab[:, :, 0] + p['gdn_dt_bias'])
    beta = jax.nn.sigmoid(ab[:, :, 1]).transpose(0, 2, 1, 3)
    nch = length // GDN_CHUNK
    g = g.transpose(0, 2, 1, 3).reshape(bsz, 2, nch, GDN_CHUNK, nh)
    g = jnp.stack([jnp.cumsum(g[:, 0], axis=2),
                   jnp.flip(jnp.cumsum(jnp.flip(g[:, 1], axis=2), axis=2), axis=2)], axis=1)
    gcol = g.reshape(bsz, 2, length, nh)
    grow = g.transpose(0, 1, 4, 2, 3)[:, :, :, :, None, :]
    o_f, o_b, s_out = gdn_scan(qkv, gcol, beta, grow, s0, nh)
    return _proj(gdn_post(o_f, o_b, proj, p['gdn_norm_w']), w_out, out_dtype=BF16), s_out


FFN_PAD = 1024


def _ffn_weights(p):
    f = p['ffn_b_dw'].shape[0]
    fp = -(-f // FFN_PAD) * FFN_PAD if f > FFN_PAD else f
    pad = lambda a, axis: jnp.pad(a, [(0, fp - f) if ax == axis else (0, 0) for ax in range(a.ndim)])
    w_up = p['ffn_w_up'].astype(BF16)
    w_up = jnp.concatenate([pad(w_up[:, :f], 1), pad(w_up[:, f:], 1)], axis=1)
    return w_up, pad(p['ffn_w_down'].astype(BF16), 0), pad(p['ffn_w_dw'], 2), pad(p['ffn_b_dw'], 0)


def _conv_glu(h, ffn_w, rows, cols):
    w_up, w_down, w_dw, b_dw = ffn_w
    assert h.shape[1] == rows * cols
    return _proj(ffn_act(_proj(h, w_up, out_dtype=BF16), w_dw, b_dw, cols), w_down, out_dtype=BF16)


def kernel(x, c, ctx, c_ctx, l0_w_ada, l0_b_ada, l0_ln1_g, l0_ln1_b, l0_ln2_g, l0_ln2_b, l0_hy_w_in, l0_hy_b_in, l0_hy_w_short, l0_hy_b_short, l0_hy_f_w1, l0_hy_f_b1, l0_hy_f_w2, l0_hy_f_b2, l0_hy_f_w3, l0_hy_f_b3, l0_hy_f_wout, l0_hy_f_freq, l0_hy_skip, l0_hy_w_out, l0_hy_b_out, l0_ffn_w_up, l0_ffn_w_dw, l0_ffn_b_dw, l0_ffn_w_down, l1_w_ada, l1_b_ada, l1_ln1_g, l1_ln1_b, l1_ln2_g, l1_ln2_b, l1_gdn_w_in, l1_gdn_w_conv, l1_gdn_a_log, l1_gdn_dt_bias, l1_gdn_norm_w, l1_gdn_w_out, l1_ffn_w_up, l1_ffn_w_dw, l1_ffn_b_dw, l1_ffn_w_down):
    layers = (
        dict(w_ada=l0_w_ada, b_ada=l0_b_ada, ln1_g=l0_ln1_g, ln1_b=l0_ln1_b, ln2_g=l0_ln2_g, ln2_b=l0_ln2_b,
             hy_w_in=l0_hy_w_in, hy_b_in=l0_hy_b_in, hy_w_short=l0_hy_w_short, hy_b_short=l0_hy_b_short,
             hy_f_w1=l0_hy_f_w1, hy_f_b1=l0_hy_f_b1, hy_f_w2=l0_hy_f_w2, hy_f_b2=l0_hy_f_b2,
             hy_f_w3=l0_hy_f_w3, hy_f_b3=l0_hy_f_b3, hy_f_wout=l0_hy_f_wout, hy_f_freq=l0_hy_f_freq,
             hy_skip=l0_hy_skip, hy_w_out=l0_hy_w_out, hy_b_out=l0_hy_b_out,
             ffn_w_up=l0_ffn_w_up, ffn_w_dw=l0_ffn_w_dw, ffn_b_dw=l0_ffn_b_dw, ffn_w_down=l0_ffn_w_down),
        dict(w_ada=l1_w_ada, b_ada=l1_b_ada, ln1_g=l1_ln1_g, ln1_b=l1_ln1_b, ln2_g=l1_ln2_g, ln2_b=l1_ln2_b,
             gdn_w_in=l1_gdn_w_in, gdn_w_conv=l1_gdn_w_conv, gdn_a_log=l1_gdn_a_log,
             gdn_dt_bias=l1_gdn_dt_bias, gdn_norm_w=l1_gdn_norm_w, gdn_w_out=l1_gdn_w_out,
             ffn_w_up=l1_ffn_w_up, ffn_w_dw=l1_ffn_w_dw, ffn_b_dw=l1_ffn_b_dw, ffn_w_down=l1_ffn_w_down),
    )
    bsz, seq, d = x.shape
    rows = seq // GRID_W
    ctx_len = ctx.shape[1]
    cond = jnp.concatenate([c, c_ctx[None], jnp.zeros((8 - bsz - 1, d), F32)], axis=0)
    mods = []
    for p in layers:
        mod = matmul(jax.nn.silu(cond), p['w_ada'], p['b_ada'], tn=512)
        mods.append((jnp.split(mod[:bsz, None, :], 6, axis=-1),
                     jnp.split(mod[bsz:bsz + 1], 6, axis=-1)))
    h_lat = modulate(x, mods[0][0][0], mods[0][0][1])
    h_ctx = modulate(ctx, mods[0][1][0], mods[0][1][1])
    for i, p in enumerate(layers):
        last = i == DEPTH - 1
        (sh1, sc1, gt1, sh2, sc2, gt2), (csh1, csc1, cgt1, csh2, csc2, cgt2) = mods[i]
        nxt, cnxt = (None, None) if last else (mods[i + 1][0][:2], mods[i + 1][1][:2])
        if 'hy_w_in' in p:
            w_in, w_out = p['hy_w_in'].astype(BF16), p['hy_w_out'].astype(BF16)
            y_lat = _hyena_mixer(h_lat, p, w_in, w_out)
            y_ctx = None if last else _hyena_mixer(h_ctx, p, w_in, w_out)
        else:
            nh = p['gdn_a_log'].shape[1]
            w_qkvz = p['gdn_w_in'][:, :4 * d].astype(BF16)
            w_ab = p['gdn_w_in'][:, 4 * d:].astype(BF16)
            w_out = p['gdn_w_out'].astype(BF16)
            s0 = jnp.zeros((bsz, 2, nh, d // nh, d // nh), F32)
            y_ctx, s_ctx = _gdn_mixer(h_ctx, p, w_qkvz, w_ab, w_out, s0)
            y_lat, _ = _gdn_mixer(h_lat, p, w_qkvz, w_ab, w_out, s_ctx)
        ffn_w = _ffn_weights(p)
        x, h_mid = post_norm(x, y_lat, gt1, p['ln1_g'], p['ln1_b'], sh2, sc2)
        ffn = _conv_glu(h_mid, ffn_w, rows, GRID_W)
        x, h_lat = post_norm(x, ffn, gt2, p['ln2_g'], p['ln2_b'], *(nxt or ()))
        if not last:
            ctx, h_mid = post_norm(ctx, y_ctx, cgt1, p['ln1_g'], p['ln1_b'], csh2, csc2)
            ffn = _conv_glu(h_mid, ffn_w, 1, ctx_len)
            ctx, h_ctx = post_norm(ctx, ffn, cgt2, p['ln2_g'], p['ln2_b'], *cnxt)
    return x
```

```python
import functools
import math

import numpy as np
import jax
import jax.numpy as jnp
from jax import lax
from jax.experimental import pallas as pl
from jax.experimental.pallas import tpu as pltpu

F32 = jnp.float32
BF16 = jnp.bfloat16

GRID_W = 64
HY_ORDER = 2
HY_EMB_DIM = 33
HY_DECAY_TARGET = 1e-2
HY_FAST_DECAY_PCT = 0.3
HY_SLOW_DECAY_PCT = 1.5
GDN_CHUNK = 64
LN_EPS = 1e-5
RMS_EPS = 1e-6
L2_EPS = 1e-6
DEPTH = 2
DN_ALPHA = (2 * DEPTH) ** 0.25

LANES = 128
SLAB = 128
SLAB_PITCH = SLAB + 8
VMEM_LIMIT = 56 * 1024 * 1024


def _cparams(sem):
    return pltpu.CompilerParams(dimension_semantics=sem, vmem_limit_bytes=VMEM_LIMIT)


def _mm_kernel(a_ref, w_ref, b_ref, o_ref, acc_ref, *, nk):
    part = jnp.dot(a_ref[...].astype(BF16), w_ref[...].astype(BF16), preferred_element_type=F32)
    if nk == 1:
        o_ref[...] = (part + b_ref[...]).astype(o_ref.dtype)
        return
    k = pl.program_id(2)

    @pl.when(k == 0)
    def _():
        acc_ref[...] = part

    @pl.when(k > 0)
    def _():
        acc_ref[...] += part

    @pl.when(k == nk - 1)
    def _():
        o_ref[...] = (acc_ref[...] + b_ref[...]).astype(o_ref.dtype)


def _pick(n, pref):
    if n <= pref:
        return n
    t = (pref // LANES) * LANES
    while t >= LANES:
        if n % t == 0:
            return t
        t -= LANES
    return n


def matmul(a, w, bias=None, out_dtype=F32, tm=1024, tn=1024, tk=4096):
    m, kd = a.shape
    kd2, n = w.shape
    assert kd == kd2
    tm, tn, tk = _pick(m, tm), _pick(n, tn), _pick(kd, tk)
    nk = kd // tk
    if bias is None:
        bias = jnp.zeros((n,), F32)
    bias = bias.reshape(1, n).astype(F32)
    return pl.pallas_call(
        functools.partial(_mm_kernel, nk=nk),
        grid=(m // tm, n // tn, nk),
        in_specs=[pl.BlockSpec((tm, tk), lambda i, j, k: (i, k)),
                  pl.BlockSpec((tk, tn), lambda i, j, k: (k, j)),
                  pl.BlockSpec((1, tn), lambda i, j, k: (0, j))],
        out_specs=pl.BlockSpec((tm, tn), lambda i, j, k: (i, j)),
        out_shape=jax.ShapeDtypeStruct((m, n), out_dtype),
        scratch_shapes=[pltpu.VMEM((tm, tn) if nk > 1 else (8, LANES), F32)],
        compiler_params=_cparams(("parallel", "parallel", "arbitrary")),
        name="matmul",
    )(a, w, bias)


def _real_form(mat):
    return np.block([[mat.real, -mat.imag], [mat.imag, mat.real]])


@functools.lru_cache(maxsize=None)
def _two_stage_consts(n1, n2):
    n = n1 * n2
    h1 = n1 // 2
    a = np.arange(n1)
    b = np.arange(n2)
    ma = np.exp(-2j * np.pi * np.outer(a, a) / n1)
    mb = np.exp(-2j * np.pi * np.outer(b, b) / n2)
    fa_half = _real_form(ma[:, :h1])
    fa_real = np.concatenate([ma.real, ma.imag], axis=0)
    fb = _real_form(mb)
    fbi = _real_form(np.conj(mb))
    fai = _real_form(np.conj(ma)[:h1, :] / n)
    tw = np.exp(-2j * np.pi * b / n)
    tw = np.stack([np.broadcast_to(tw.real[:, None], (n2, LANES)),
                   np.broadcast_to(tw.imag[:, None], (n2, LANES))])
    return dict(fa_half=fa_half, fa_real=fa_real, fb=fb, fbi=fbi, fai=fai, tw=tw)


@functools.lru_cache(maxsize=None)
def _dense_consts(length):
    n = 2 * length
    f = np.arange(n)
    t = np.arange(length)
    fwd = np.exp(-2j * np.pi * np.outer(f, t) / n)
    fwd_full = np.exp(-2j * np.pi * np.outer(f, f) / n)
    inv = np.exp(2j * np.pi * np.outer(t, f) / n) / n
    return dict(fwd=_real_form(fwd),
                fwd_real=np.concatenate([fwd_full.real, fwd_full.imag], axis=0),
                inv=_real_form(inv))


def _cmul(ar, ai, br, bi):
    return ar * br - ai * bi, ar * bi + ai * br


def _twiddle_init(t_ref):
    t_ref[0] = jnp.ones(t_ref.shape[1:], F32)
    t_ref[1] = jnp.zeros(t_ref.shape[1:], F32)


def _twiddle_step(t_ref, tw_ref, tr, ti):
    nr, ni = _cmul(tr, ti, tw_ref[0], tw_ref[1])
    t_ref[0] = nr
    t_ref[1] = ni


def _twiddle_pair(t_ref, tw_ref):
    tr, ti = t_ref[0], t_ref[1]
    return (tr, ti), _cmul(tr, ti, tw_ref[0], tw_ref[1])


def _column_stage(mat, bufr, bufi, rows_in, rows_out, complex_in):
    def column(b):
        xr = bufr[pl.ds(b, rows_in, stride=SLAB_PITCH), :]
        if not complex_in:
            return xr
        return jnp.concatenate([xr, bufi[pl.ds(b, rows_in, stride=SLAB_PITCH), :]], axis=0)

    def body(p, c):
        x = jnp.concatenate([column(2 * p), column(2 * p + 1)], axis=1).astype(BF16)
        y = jnp.dot(mat, x, preferred_element_type=F32)
        for q in range(2):
            yq = y[:, q * LANES:(q + 1) * LANES]
            bufr[pl.ds(2 * p + q, rows_out, stride=SLAB_PITCH), :] = yq[:rows_out]
            bufi[pl.ds(2 * p + q, rows_out, stride=SLAB_PITCH), :] = yq[rows_out:]
        return c
    lax.fori_loop(0, SLAB // 2, body, 0, unroll=8)


def _filter_spectrum_kernel(feat_ref, featb_ref, wf_ref, wb_ref, delta_ref, fa_ref, fb_ref, tw_ref, o_ref,
                            bufr, bufi, t_ref, *, n1, sb):
    j = pl.program_id(1)
    pitch = SLAB_PITCH
    h1 = n1 // 2
    length = h1 * SLAB

    @pl.when(j == 0)
    def _():
        rate = delta_ref[...] * (1.0 / (length - 1))
        pos = lax.broadcasted_iota(jnp.int32, (SLAB, LANES), 0).astype(F32)
        wf = wf_ref[...].astype(BF16)
        wb = wb_ref[...].astype(BF16)

        def load(a, c):
            src = pl.multiple_of(a * SLAB, SLAB)
            m = pos + lax.convert_element_type(a * SLAB, F32)
            hf = jnp.dot(feat_ref[pl.ds(src, SLAB), :].astype(BF16), wf, preferred_element_type=F32)
            hb = jnp.dot(featb_ref[pl.ds(src, SLAB), :].astype(BF16), wb, preferred_element_type=F32)
            bufr[pl.ds(pl.multiple_of(a * pitch, 8), SLAB), :] = hf * jnp.exp(-m * rate)
            bufr[pl.ds(pl.multiple_of((h1 + a) * pitch, 8), SLAB), :] = hb * jnp.exp((m - length) * rate)
            return c
        lax.fori_loop(0, h1, load, 0, unroll=4)
        tap0 =jnp.dot(feat_ref[0:8, :].astype(BF16), wb, preferred_element_type=F32)
        bufr[0:1, :] = bufr[0:1, :] + tap0[0:1]
        _column_stage(fa_ref[...], bufr, bufi, n1, n1, False)
        _twiddle_init(t_ref)

    fb = fb_ref[...]

    def stage_b(s, c):
        twiddles = _twiddle_pair(t_ref, tw_ref)
        cols = []
        for q in range(2):
            row = pl.multiple_of((j * sb + 2 * s + q) * pitch, 8)
            yr, yi = _cmul(bufr[pl.ds(row, SLAB), :], bufi[pl.ds(row, SLAB), :], *twiddles[q])
            cols.append(jnp.concatenate([yr, yi], axis=0))
        spec = jnp.dot(fb, jnp.concatenate(cols, axis=1).astype(BF16), preferred_element_type=F32)
        for q in range(2):
            out = pl.multiple_of((2 * s + q) * SLAB, SLAB)
            o_ref[0, pl.ds(out, SLAB), :] = spec[:SLAB, q * LANES:(q + 1) * LANES]
            o_ref[1, pl.ds(out, SLAB), :] = spec[SLAB:, q * LANES:(q + 1) * LANES]
        _twiddle_step(t_ref, tw_ref, *twiddles[1])
        return c
    lax.fori_loop(0, sb // 2, stage_b, 0, unroll=4)


def _slab_blocks(n1):
    return 8 if n1 % 8 == 0 and n1 >= 16 else 1


def filter_spectrum(feat, feat_circ, w_out, fwd_group, bwd_group, deltas):
    length, fh = feat.shape
    d = deltas.shape[0]
    n = 2 * length
    n1 = n // SLAB
    kb = _slab_blocks(n1)
    sb = n1 // kb
    c = _two_stage_consts(n1, SLAB)
    fa = jnp.asarray(c["fa_real"], BF16)
    fb = jnp.asarray(c["fb"], BF16)
    tw = jnp.asarray(c["tw"], F32)
    const = lambda shape: pl.BlockSpec(shape, lambda i, j: (0,) * len(shape))
    return pl.pallas_call(
        functools.partial(_filter_spectrum_kernel, n1=n1, sb=sb),
        grid=(d // LANES, kb),
        in_specs=[const(feat.shape), const(feat.shape),
                  pl.BlockSpec((fh, LANES), lambda i, j: (0, fwd_group * (d // LANES) + i)),
                  pl.BlockSpec((fh, LANES), lambda i, j: (0, bwd_group * (d // LANES) + i)),
                  pl.BlockSpec((1, LANES), lambda i, j: (0, i)),
                  const(fa.shape), const(fb.shape), const(tw.shape)],
        out_specs=pl.BlockSpec((2, sb * SLAB, LANES), lambda i, j: (0, j, i)),
        out_shape=jax.ShapeDtypeStruct((2, n, d), F32),
        scratch_shapes=[pltpu.VMEM((n1 * SLAB_PITCH, LANES), F32),
                        pltpu.VMEM((n1 * SLAB_PITCH, LANES), F32),
                        pltpu.VMEM((2, SLAB, LANES), F32)],
        compiler_params=_cparams(("parallel", "arbitrary")),
        name="filter_spectrum",
    )(feat, feat_circ, w_out, w_out, deltas.reshape(1, d), fa, fb, tw)


def _fftconv_kernel(z_ref, g_ref, ks_ref, skip_ref, fa_ref, fb_ref, fbi_ref, fai_ref, tw_ref, o_ref,
                    bufr, bufi, t_ref, *, n1, sb, kb):
    j = pl.program_id(1)
    pitch = SLAB_PITCH
    h1 = n1 // 2

    @pl.when(j == 0)
    def _():
        def load(a, c):
            src = pl.multiple_of(a * SLAB, SLAB)
            dst = pl.multiple_of(a * pitch, 8)
            bufr[pl.ds(dst, SLAB), :] = z_ref[0, pl.ds(src, SLAB), :].astype(F32)
            bufi[pl.ds(dst, SLAB), :] = z_ref[1, pl.ds(src, SLAB), :].astype(F32)
            return c
        lax.fori_loop(0, h1, load, 0)
        _column_stage(fa_ref[...], bufr, bufi, h1, n1, True)
        _twiddle_init(t_ref)

    fb = fb_ref[...]
    fbi = fbi_ref[...]

    def stage_b(s, c):
        twiddles = _twiddle_pair(t_ref, tw_ref)
        rows = [pl.multiple_of((j * sb + 2 * s + q) * pitch, 8) for q in range(2)]
        cols = []
        for q in range(2):
            yr, yi = _cmul(bufr[pl.ds(rows[q], SLAB), :], bufi[pl.ds(rows[q], SLAB), :], *twiddles[q])
            cols.append(jnp.concatenate([yr, yi], axis=0))
        spec = jnp.dot(fb, jnp.concatenate(cols, axis=1).astype(BF16), preferred_element_type=F32)
        prods = []
        for q in range(2):
            krow = pl.multiple_of((2 * s + q) * SLAB, SLAB)
            sq = spec[:, q * LANES:(q + 1) * LANES]
            pr, pi = _cmul(sq[:SLAB], sq[SLAB:], ks_ref[0, pl.ds(krow, SLAB), :], ks_ref[1, pl.ds(krow, SLAB), :])
            prods.append(jnp.concatenate([pr, pi], axis=0))
        back = jnp.dot(fbi, jnp.concatenate(prods, axis=1).astype(BF16), preferred_element_type=F32)
        for q in range(2):
            bq = back[:, q * LANES:(q + 1) * LANES]
            ur, ui = _cmul(bq[:SLAB], bq[SLAB:], twiddles[q][0], -twiddles[q][1])
            bufr[pl.ds(rows[q], SLAB), :] = ur
            bufi[pl.ds(rows[q], SLAB), :] = ui
        _twiddle_step(t_ref, tw_ref, *twiddles[1])
        return c
    lax.fori_loop(0, sb // 2, stage_b, 0, unroll=4)

    @pl.when(j == kb - 1)
    def _():
        _column_stage(fai_ref[...], bufr, bufi, n1, h1, True)
        skip = skip_ref[...]

        def store(a, c):
            dst = pl.multiple_of(a * SLAB, SLAB)
            src = pl.multiple_of(a * pitch, 8)
            for bi, buf in enumerate((bufr, bufi)):
                zz = z_ref[bi, pl.ds(dst, SLAB), :].astype(F32)
                gg = g_ref[bi, pl.ds(dst, SLAB), :].astype(F32)
                o_ref[bi, pl.ds(dst, SLAB), :] = (gg * (buf[pl.ds(src, SLAB), :] + zz * skip)).astype(o_ref.dtype)
            return c
        lax.fori_loop(0, h1, store, 0)


def fftconv_gated(z, gate, kspec, skip, z_col=0, gate_col=0, out_dtype=BF16):
    bsz, length, _ = z.shape
    d = skip.shape[0]
    assert bsz == 2
    n1 = 2 * length // SLAB
    kb = _slab_blocks(n1)
    sb = n1 // kb
    c = _two_stage_consts(n1, SLAB)
    fa = jnp.asarray(c["fa_half"], BF16)
    fb = jnp.asarray(c["fb"], BF16)
    fbi = jnp.asarray(c["fbi"], BF16)
    fai = jnp.asarray(c["fai"], BF16)
    tw = jnp.asarray(c["tw"], F32)
    const = lambda shape: pl.BlockSpec(shape, lambda i, j: (0,) * len(shape))
    seq = lambda col=0: pl.BlockSpec((2, length, LANES), lambda i, j: (0, 0, col // LANES + i))
    return pl.pallas_call(
        functools.partial(_fftconv_kernel, n1=n1, sb=sb, kb=kb),
        grid=(d // LANES, kb),
        in_specs=[seq(z_col), seq(gate_col),
                  pl.BlockSpec((2, sb * SLAB, LANES), lambda i, j: (0, j, i)),
                  pl.BlockSpec((1, LANES), lambda i, j: (0, i)),
                  const(fa.shape), const(fb.shape), const(fbi.shape), const(fai.shape), const(tw.shape)],
        out_specs=seq(),
        out_shape=jax.ShapeDtypeStruct((2, length, d), out_dtype),
        scratch_shapes=[pltpu.VMEM((n1 * SLAB_PITCH, LANES), F32),
                        pltpu.VMEM((n1 * SLAB_PITCH, LANES), F32),
                        pltpu.VMEM((2, SLAB, LANES), F32)],
        compiler_params=_cparams(("parallel", "arbitrary")),
        name="fftconv",
    )(z, gate, kspec, skip.reshape(1, d), fa, fb, fbi, fai, tw)


def _dense_conv_kernel(z_ref, g_ref, k_ref, skip_ref, fwd_ref, fwdk_ref, inv_ref, o_ref, *, length):
    n = 2 * length
    x = jnp.concatenate([z_ref[0], z_ref[1]], axis=0)
    spec = jnp.dot(fwd_ref[...], x.astype(BF16), preferred_element_type=F32)
    kspec = jnp.dot(fwdk_ref[...], k_ref[...].astype(BF16), preferred_element_type=F32)
    pr, pi = _cmul(spec[:n], spec[n:], kspec[:n], kspec[n:])
    y = jnp.dot(inv_ref[...], jnp.concatenate([pr, pi], axis=0).astype(BF16), preferred_element_type=F32)
    skip = skip_ref[...]
    for bi in range(2):
        yy = y[bi * length:(bi + 1) * length]
        o_ref[bi] = (g_ref[bi].astype(F32) * (yy + z_ref[bi].astype(F32) * skip)).astype(o_ref.dtype)


def dense_conv_gated(z, gate, k_time, skip, z_col=0, gate_col=0, out_dtype=BF16):
    bsz, length, _ = z.shape
    d = skip.shape[0]
    assert bsz == 2
    c = _dense_consts(length)
    fwd = jnp.asarray(c["fwd"], BF16)
    fwdk = jnp.asarray(c["fwd_real"], BF16)
    inv = jnp.asarray(c["inv"], BF16)
    tile = 2 * LANES if d % (2 * LANES) == 0 else LANES
    const = lambda shape: pl.BlockSpec(shape, lambda i: (0,) * len(shape))
    seq = lambda col=0: pl.BlockSpec((2, length, tile), lambda i: (0, 0, col // tile + i))
    return pl.pallas_call(
        functools.partial(_dense_conv_kernel, length=length),
        grid=(d // tile,),
        in_specs=[seq(z_col), seq(gate_col), pl.BlockSpec((2 * length, tile), lambda i: (0, i)),
                  pl.BlockSpec((1, tile), lambda i: (0, i)),
                  const(fwd.shape), const(fwdk.shape), const(inv.shape)],
        out_specs=seq(),
        out_shape=jax.ShapeDtypeStruct((2, length, d), out_dtype),
        compiler_params=_cparams(("parallel",)),
        name="dense_conv",
    )(z, gate, k_time, skip.reshape(1, d), fwd, fwdk, inv)


def _bdot(a, b):
    return lax.dot_general(a.astype(BF16), b.astype(BF16), (((2,), (1,)), ((0,), (0,))),
                           preferred_element_type=F32)


def _bdot_nt(a, b):
    return lax.dot_general(a.astype(BF16), b.astype(BF16), (((2,), (2,)), ((0,), (0,))),
                           preferred_element_type=F32)


def _dot(a, b):
    return jnp.dot(a.astype(BF16), b.astype(BF16), preferred_element_type=F32)


def _dot_tn(a, b):
    return lax.dot_general(a.astype(BF16), b.astype(BF16), (((0,), (0,)), ((), ())), preferred_element_type=F32)


def _gdn_kernel(qf_ref, kf_ref, vf_ref, qb_ref, kb_ref, vb_ref, gcf_ref, bcf_ref, gcb_ref, bcb_ref,
                grf_ref, grb_ref, s0_ref, of_ref, ob_ref, sout_ref,
                s_ref, u_ref, w_ref, qg_ref, at_ref, gl_ref, mw_ref, n_ref, sc_ref, *, r, nb, hb, nh):
    cs = GDN_CHUNK
    head0 = pl.program_id(1) * hb
    j = pl.program_id(2)

    @pl.when(j == 0)
    def _():
        s_ref[...] = s0_ref[0]

    ii = lax.broadcasted_iota(jnp.int32, (cs, cs), 0)
    jj = lax.broadcasted_iota(jnp.int32, (cs, cs), 1)
    eye = (ii == jj).astype(F32)
    blk_same = [jnp.right_shift(ii, lvl) == jnp.right_shift(jj, lvl) for lvl in range(int(math.log2(cs)) + 1)]
    lane_head = lax.broadcasted_iota(jnp.int32, (r * cs, nh), 1)

    for dr in range(2):
        q_ref, k_ref, v_ref = (qf_ref, kf_ref, vf_ref) if dr == 0 else (qb_ref, kb_ref, vb_ref)
        gcol_ref, bcol_ref = (gcf_ref, bcf_ref) if dr == 0 else (gcb_ref, bcb_ref)
        grow_ref = grf_ref if dr == 0 else grb_ref
        incl = (ii >= jj) if dr == 0 else (ii <= jj)
        strict = (ii > jj) if dr == 0 else (ii < jj)
        last = cs - 1 if dr == 0 else 0
        def heads(ref):
            return jnp.concatenate([ref[0, :, hh * LANES:(hh + 1) * LANES].astype(F32).reshape(r, cs, LANES)
                                    for hh in range(hb)], axis=0)

        def head_cols(ref):
            return jnp.concatenate(
                [jnp.sum(jnp.where(lane_head == head0 + hh, ref[0, 0], 0.0), axis=1, keepdims=True).reshape(r, cs, 1)
                 for hh in range(hb)], axis=0)
        q, k, v = heads(q_ref), heads(k_ref), heads(v_ref)
        gc, bc = head_cols(gcol_ref), head_cols(bcol_ref)
        gr = grow_ref[0, 0].reshape(hb * r, 1, cs)
        g_last = gr[:, :, last:last + 1]
        decay = jnp.where(incl, jnp.exp(jnp.where(incl, gc - gr, 0.0)), 0.0)
        kbeta = k * bc
        a_mat = jnp.where(strict, _bdot_nt(kbeta, k) * decay, 0.0).astype(BF16).astype(F32)
        t = eye - jnp.where(blk_same[1] & ~blk_same[0], a_mat, 0.0)
        for lvl in range(1, len(blk_same) - 1):
            x = jnp.where(blk_same[lvl + 1] & ~blk_same[lvl], a_mat, 0.0)
            t = t - _bdot(t, _bdot(x, t))
        eg = jnp.exp(gc)
        sol = _bdot(t, jnp.concatenate([v * bc, kbeta * eg], axis=2))
        u, w = sol[:, :, :LANES], sol[:, :, LANES:]
        per_head = lambda a: a.reshape((hb, r) + a.shape[1:])
        u_ref[dr] = per_head(u)
        w_ref[dr] = per_head(w)
        at_ref[dr] = per_head(jnp.where(incl, _bdot_nt(q, k) * decay, 0.0))
        qg_ref[dr] = per_head(q * eg)
        kg_t = jnp.swapaxes(k * jnp.exp(g_last - gc), 1, 2)
        mw_ref[dr] = per_head(-_bdot(kg_t, w))
        n_ref[dr] = per_head(_bdot(kg_t, u))
        gl_ref[dr] = per_head(jnp.broadcast_to(jnp.exp(g_last), (hb * r, 8, LANES)))

    def scan(p, carry):
        for dr in range(2):
            c = p if dr == 0 else r - 1 - p
            for hh in range(hb):
                s = s_ref[dr, hh]
                sc_ref[dr, hh, c] = s
                s_ref[dr, hh] = s * gl_ref[dr, hh, c][0:1, :] + _dot(mw_ref[dr, hh, c], s) + n_ref[dr, hh, c]
        return carry
    lax.fori_loop(0, r, scan, 0, unroll=2)

    for dr in range(2):
        o_ref = of_ref if dr == 0 else ob_ref
        flat = lambda ref: ref[dr].reshape((hb * r,) + ref.shape[3:])
        sc = flat(sc_ref)
        v_new = flat(u_ref) - _bdot(flat(w_ref), sc)
        o = _bdot(flat(qg_ref), sc) + _bdot(flat(at_ref), v_new)
        for hh in range(hb):
            o_ref[0, :, hh * LANES:(hh + 1) * LANES] = (
                o[hh * r:(hh + 1) * r].reshape(r * cs, LANES).astype(o_ref.dtype))

    @pl.when(j == nb - 1)
    def _():
        sout_ref[0] = s_ref[...]


def gdn_scan(qkv, gcol, bcol, grow, s0, nh):
    bsz, length, d3 = qkv.shape
    d = d3 // 3
    dh = d // nh
    assert dh == LANES
    hb = 4 if nh % 4 == 0 else 2 if nh % 2 == 0 else 1
    ng = nh // hb
    nch = length // GDN_CHUNK
    r = 8 if nch % 8 == 0 else nch
    nb = nch // r
    rows = r * GDN_CHUNK
    blk = lambda j, rev: nb - 1 - j if rev else j
    seq = lambda off, rev: pl.BlockSpec((1, rows, hb * LANES), lambda b, g, j: (b, blk(j, rev), off * ng + g))
    col = lambda rev: pl.BlockSpec((1, 1, rows, nh), lambda b, g, j: (b, rev, blk(j, rev), 0))
    row = lambda rev: pl.BlockSpec((1, 1, hb, r, 1, GDN_CHUNK), lambda b, g, j: (b, rev, g, blk(j, rev), 0, 0))
    out = lambda rev: pl.BlockSpec((1, rows, hb * LANES), lambda b, g, j: (b, blk(j, rev), g))
    state = pl.BlockSpec((1, 2, hb, dh, dh), lambda b, g, j: (b, 0, g, 0, 0))
    per_chunk = lambda *shape: pltpu.VMEM((2, hb, r) + shape, F32)
    return pl.pallas_call(
        functools.partial(_gdn_kernel, r=r, nb=nb, hb=hb, nh=nh),
        grid=(bsz, ng, nb),
        in_specs=[seq(0, 0), seq(1, 0), seq(2, 0), seq(0, 1), seq(1, 1), seq(2, 1),
                  col(0), col(0), col(1), col(1), row(0), row(1), state],
        out_specs=[out(0), out(1), state],
        out_shape=[jax.ShapeDtypeStruct((bsz, length, d), BF16),
                   jax.ShapeDtypeStruct((bsz, length, d), BF16),
                   jax.ShapeDtypeStruct((bsz, 2, nh, dh, dh), F32)],
        scratch_shapes=[pltpu.VMEM((2, hb, dh, dh), F32),
                        per_chunk(GDN_CHUNK, LANES), per_chunk(GDN_CHUNK, LANES), per_chunk(GDN_CHUNK, LANES),
                        per_chunk(GDN_CHUNK, GDN_CHUNK), per_chunk(8, LANES),
                        per_chunk(dh, dh), per_chunk(dh, dh), per_chunk(dh, dh)],
        compiler_params=_cparams(("parallel", "parallel", "arbitrary")),
        name="gdn_scan",
    )(qkv, qkv, qkv, qkv, qkv, qkv, gcol, bcol, gcol, bcol, grow, grow, s0)


ROW_TILE = 256
HALO = 16


def _per_batch(v, bsz):
    return jnp.broadcast_to(v.reshape(-1, 1, v.shape[-1]), (bsz, 1, v.shape[-1]))


def _modulate_kernel(x_ref, sh_ref, sc_ref, h_ref):
    h_ref[0] = (x_ref[0] * (1.0 + sc_ref[0]) + sh_ref[0]).astype(h_ref.dtype)


def modulate(x, shift, scale):
    bsz, length, d = x.shape
    tr = _pick(length, ROW_TILE)
    row = pl.BlockSpec((1, tr, d), lambda b, i: (b, i, 0))
    vec = pl.BlockSpec((1, 1, d), lambda b, i: (b, 0, 0))
    return pl.pallas_call(
        _modulate_kernel, grid=(bsz, length // tr), in_specs=[row, vec, vec], out_specs=row,
        out_shape=jax.ShapeDtypeStruct(x.shape, BF16), compiler_params=_cparams(("parallel", "parallel")),
        name="modulate",
    )(x, _per_batch(shift, bsz), _per_batch(scale, bsz))


def _post_norm_kernel(x_ref, y_ref, gate_ref, g_ref, b_ref, sh_ref, sc_ref, xo_ref, *h_ref):
    v = DN_ALPHA * x_ref[0] + gate_ref[0] * y_ref[0].astype(F32)
    cen = v - jnp.mean(v, axis=1, keepdims=True)
    out = cen * lax.rsqrt(jnp.mean(cen * cen, axis=1, keepdims=True) + LN_EPS) * g_ref[...] + b_ref[...]
    xo_ref[0] = out
    if h_ref:
        h_ref[0][0] = (out * (1.0 + sc_ref[0]) + sh_ref[0]).astype(BF16)


def post_norm(x, y, gate, g, b, shift=None, scale=None):
    bsz, length, d = x.shape
    with_h = shift is not None
    if not with_h:
        shift = scale = jnp.zeros((1, d), F32)
    tr = _pick(length, ROW_TILE)
    row = pl.BlockSpec((1, tr, d), lambda b, i: (b, i, 0))
    vec = pl.BlockSpec((1, 1, d), lambda b, i: (b, 0, 0))
    par = pl.BlockSpec((1, d), lambda b, i: (0, 0))
    out = pl.pallas_call(
        _post_norm_kernel, grid=(bsz, length // tr),
        in_specs=[row, row, vec, par, par, vec, vec],
        out_specs=[row, row] if with_h else [row],
        out_shape=[jax.ShapeDtypeStruct(x.shape, F32)] + ([jax.ShapeDtypeStruct(x.shape, BF16)] if with_h else []),
        compiler_params=_cparams(("parallel", "parallel")), name="post_norm",
    )(x, y, _per_batch(gate, bsz), g.reshape(1, d), b.reshape(1, d), _per_batch(shift, bsz), _per_batch(scale, bsz))
    return out if with_h else (out[0], None)


def _seq_neighbours(x, prev_ref, next_ref, n_steps):
    i = pl.program_id(2)
    rows = x.shape[0]
    row = lax.broadcasted_iota(jnp.int32, x.shape, 0)
    before = jnp.where(i == 0, 0.0, prev_ref[0, HALO - 1:HALO, :].astype(F32))
    after = jnp.where(i == n_steps - 1, 0.0, next_ref[0, 0:1, :].astype(F32))
    prev = jnp.where(row == 0, before, pltpu.roll(x, 1, 0))
    nxt = jnp.where(row == rows - 1, after, pltpu.roll(x, rows - 1, 0))
    return prev, nxt


def _seq_specs(tr, tc, length):
    per = tr // HALO
    last = length // HALO - 1
    main = pl.BlockSpec((1, tr, tc), lambda b, j, i: (b, i, j))
    prev = pl.BlockSpec((1, HALO, tc), lambda b, j, i: (b, jnp.maximum(i * per - 1, 0), j))
    nxt = pl.BlockSpec((1, HALO, tc), lambda b, j, i: (b, jnp.minimum((i + 1) * per, last), j))
    return main, prev, nxt


def _hy_short_conv_kernel(x_ref, p_ref, n_ref, w_ref, b_ref, o_ref, *, n_steps):
    x = x_ref[0].astype(F32)
    prev, nxt = _seq_neighbours(x, p_ref, n_ref, n_steps)
    o_ref[0] = (prev * w_ref[0:1, :] + x * w_ref[1:2, :] + nxt * w_ref[2:3, :] + b_ref[...]).astype(o_ref.dtype)


def hy_short_conv(x, w, b):
    bsz, length, ch = x.shape
    tr, tc = _pick(length, 2 * ROW_TILE), _pick(ch, 1024)
    main, prev, nxt = _seq_specs(tr, tc, length)
    return pl.pallas_call(
        functools.partial(_hy_short_conv_kernel, n_steps=length // tr),
        grid=(bsz, ch // tc, length // tr),
        in_specs=[main, prev, nxt, pl.BlockSpec((3, tc), lambda b_, j, i: (0, j)),
                  pl.BlockSpec((1, tc), lambda b_, j, i: (0, j))],
        out_specs=main, out_shape=jax.ShapeDtypeStruct(x.shape, BF16),
        compiler_params=_cparams(("parallel", "parallel", "parallel")), name="hy_short_conv",
    )(x, x, x, w, b.reshape(1, ch))


def _gdn_pre_kernel(x_ref, p_ref, n_ref, w_ref, o_ref, *, n_steps, q_blocks, qk_blocks, q_scale):
    x = x_ref[0].astype(F32)
    prev, nxt = _seq_neighbours(x, p_ref, n_ref, n_steps)
    y = prev * w_ref[0:1, :] + x * w_ref[1:2, :] + nxt * w_ref[2:3, :]
    y = y * jax.nn.sigmoid(y)
    j = pl.program_id(1)
    scale = jnp.where(j < q_blocks, q_scale, 1.0)
    for h in range(y.shape[1] // LANES):
        lanes = slice(h * LANES, (h + 1) * LANES)
        yh = y[:, lanes]
        normed = yh * (lax.rsqrt(jnp.sum(yh * yh, axis=1, keepdims=True) + L2_EPS) * scale)
        o_ref[0, :, lanes] = jnp.where(j < qk_blocks, normed, yh).astype(o_ref.dtype)


def gdn_pre(proj, w_conv, d, dh):
    bsz, length, _ = proj.shape
    tr, tc = _pick(length, 2 * ROW_TILE), _pick(d, 1024)
    main, prev, nxt = _seq_specs(tr, tc, length)
    return pl.pallas_call(
        functools.partial(_gdn_pre_kernel, n_steps=length // tr, q_blocks=d // tc, qk_blocks=2 * d // tc,
                          q_scale=dh ** -0.5),
        grid=(bsz, 3 * d // tc, length // tr),
        in_specs=[main, prev, nxt, pl.BlockSpec((3, tc), lambda b_, j, i: (0, j))],
        out_specs=main, out_shape=jax.ShapeDtypeStruct((bsz, length, 3 * d), BF16),
        compiler_params=_cparams(("parallel", "parallel", "parallel")), name="gdn_pre",
    )(proj, proj, proj, w_conv)


def _gdn_post_kernel(of_ref, ob_ref, z_ref, nw_ref, o_ref):
    o = of_ref[0].astype(F32) + ob_ref[0].astype(F32)
    z = z_ref[0].astype(F32)
    gate = z * jax.nn.sigmoid(z) * nw_ref[...]
    for h in range(o.shape[1] // LANES):
        lanes = slice(h * LANES, (h + 1) * LANES)
        oh = o[:, lanes]
        rms = lax.rsqrt(jnp.mean(oh * oh, axis=1, keepdims=True) + RMS_EPS)
        o_ref[0, :, lanes] = (oh * rms * gate[:, lanes]).astype(o_ref.dtype)


def gdn_post(o_f, o_b, proj, norm_w):
    bsz, length, d = o_f.shape
    tr, tc = _pick(length, 2 * ROW_TILE), _pick(d, 512)
    z_off = (proj.shape[2] - d) // tc
    row = pl.BlockSpec((1, tr, tc), lambda b, j, i: (b, i, j))
    nw = jnp.tile(norm_w, tc // norm_w.shape[0]).reshape(1, tc)
    return pl.pallas_call(
        _gdn_post_kernel, grid=(bsz, d // tc, length // tr),
        in_specs=[row, row, pl.BlockSpec((1, tr, tc), lambda b, j, i: (b, i, z_off + j)),
                  pl.BlockSpec((1, tc), lambda b, j, i: (0, 0))],
        out_specs=row, out_shape=jax.ShapeDtypeStruct(o_f.shape, BF16),
        compiler_params=_cparams(("parallel", "parallel", "parallel")), name="gdn_post",
    )(o_f, o_b, proj, nw)


def _ffn_act_kernel(g_ref, v_ref, up_ref, dn_ref, w_ref, b_ref, o_ref, *, n_steps, gw):
    i = pl.program_id(2)
    above = jnp.where(i == 0, 0.0, up_ref[0].astype(F32))
    below = jnp.where(i == n_steps - 1, 0.0, dn_ref[0].astype(F32))
    x = jnp.concatenate([above, g_ref[0].astype(F32), below], axis=0)
    n = x.shape[0]
    col = lax.rem(lax.broadcasted_iota(jnp.int32, x.shape, 0), gw)
    left = jnp.where(col == 0, 0.0, pltpu.roll(x, 1, 0))
    right = jnp.where(col == gw - 1, 0.0, pltpu.roll(x, n - 1, 0))
    rows = g_ref.shape[1]
    acc = b_ref[...]
    for di in range(3):
        rs = slice(di * gw, di * gw + rows)
        acc = acc + (left[rs] * w_ref[3 * di:3 * di + 1, :] + x[rs] * w_ref[3 * di + 1:3 * di + 2, :]
                     + right[rs] * w_ref[3 * di + 2:3 * di + 3, :])
    gelu = 0.5 * acc * (1.0 + lax.erf(acc * (2.0 ** -0.5)))
    o_ref[0] = (gelu * v_ref[0].astype(F32)).astype(o_ref.dtype)


def ffn_act(up, w_dw, b_dw, gw):
    bsz, length, f2 = up.shape
    f = f2 // 2
    tc = _pick(f, 1024)
    tr = gw * max(1, min(length // gw, 2 * ROW_TILE // gw))
    per = tr // gw
    last = length // gw - 1
    main = lambda off: pl.BlockSpec((1, tr, tc), lambda b, j, i: (b, i, off + j))
    halo_up = pl.BlockSpec((1, gw, tc), lambda b, j, i: (b, jnp.maximum(i * per - 1, 0), j))
    halo_dn = pl.BlockSpec((1, gw, tc), lambda b, j, i: (b, jnp.minimum((i + 1) * per, last), j))
    return pl.pallas_call(
        functools.partial(_ffn_act_kernel, n_steps=length // tr, gw=gw),
        grid=(bsz, f // tc, length // tr),
        in_specs=[main(0), main(f // tc), halo_up, halo_dn,
                  pl.BlockSpec((9, tc), lambda b, j, i: (0, j)), pl.BlockSpec((1, tc), lambda b, j, i: (0, j))],
        out_specs=main(0), out_shape=jax.ShapeDtypeStruct((bsz, length, f), BF16),
        compiler_params=_cparams(("parallel", "parallel", "parallel")), name="ffn_act",
    )(up, up, up, up, w_dw.reshape(9, f), b_dw.reshape(1, f))


def _proj(h, w, bias=None, out_dtype=F32, **tiles):
    bsz, length, kd = h.shape
    return matmul(h.reshape(bsz * length, kd).astype(BF16), w, bias, out_dtype, **tiles).reshape(bsz, length, -1)


def _hyena_filter_features(length, p, d):
    t = jnp.linspace(0.0, 1.0, length, dtype=F32)[:, None]
    bands = (HY_EMB_DIM - 1) // 2
    ang = 2.0 * math.pi * jnp.arange(length, dtype=F32)[:, None] / length
    f = jnp.linspace(1e-4, bands - 1, bands, dtype=F32)[None, :]
    z = jnp.concatenate([t, jnp.cos(f * ang), -jnp.sin(f * ang)], axis=-1)
    freq = p['hy_f_freq']
    h = jnp.sin(freq * (z @ p['hy_f_w1'] + p['hy_f_b1']))
    h = jnp.sin(freq * (h @ p['hy_f_w2'] + p['hy_f_b2']))
    h = jnp.sin(freq * (h @ p['hy_f_w3'] + p['hy_f_b3']))
    max_decay = math.log(HY_DECAY_TARGET) / HY_FAST_DECAY_PCT
    min_decay = math.log(HY_DECAY_TARGET) / HY_SLOW_DECAY_PCT
    deltas = jnp.abs(jnp.linspace(min_decay, max_decay, d, dtype=F32))
    return h, jnp.concatenate([jnp.zeros_like(h[:1]), h[:0:-1]], axis=0), deltas


def _windowed_filter(feat, feat_circ, w_fwd, w_bwd, deltas):
    length = feat.shape[0]
    m = jnp.arange(length, dtype=F32)[:, None]
    rate = deltas / (length - 1)
    head = matmul(feat, w_fwd) * jnp.exp(-m * rate)
    tail = matmul(feat_circ, w_bwd) * jnp.exp((m - length) * rate)
    return jnp.concatenate([head[:1] + matmul(feat[:8], w_bwd)[:1], head[1:], tail], axis=0)


def _hyena_mixer(h, p, w_in, w_out):
    bsz, length, d = h.shape
    u = hy_short_conv(_proj(h, w_in, p['hy_b_in'], out_dtype=BF16), p['hy_w_short'], p['hy_b_short'])
    feat, feat_circ, deltas = _hyena_filter_features(length, p, d)
    w_filt = p['hy_f_wout']
    z, z_col = u, HY_ORDER * d
    two_stage = (2 * length) % (16 * SLAB) == 0
    for o in range(HY_ORDER):
        if two_stage:
            kspec = filter_spectrum(feat, feat_circ, w_filt, 2 * o, 2 * o + 1, deltas)
            z = fftconv_gated(z, u, kspec, p['hy_skip'][o], z_col, o * d)
        else:
            k_time = _windowed_filter(feat, feat_circ, w_filt[:, 2 * o * d:(2 * o + 1) * d],
                                      w_filt[:, (2 * o + 1) * d:(2 * o + 2) * d], deltas)
            z = dense_conv_gated(z, u, k_time, p['hy_skip'][o], z_col, o * d)
        z_col = 0
    return _proj(z, w_out, p['hy_b_out'], out_dtype=BF16)


def _gdn_mixer(h, p, w_qkvz, w_ab, w_out, s0):
    bsz, length, d = h.shape
    nh = p['gdn_a_log'].shape[1]
    dh = d // nh
    proj = _proj(h, w_qkvz, out_dtype=BF16)
    ab = _proj(h, w_ab).reshape(bsz, length, 2, 2, nh)
    qkv = gdn_pre(proj, p['gdn_w_conv'], d, dh)
    g =-jnp.exp(p['gdn_a_log']) * jax.nn.softplus(ab[:, :, 0] + p['gdn_dt_bias'])
    beta = jax.nn.sigmoid(ab[:, :, 1]).transpose(0, 2, 1, 3)
    nch = length // GDN_CHUNK
    g = g.transpose(0, 2, 1, 3).reshape(bsz, 2, nch, GDN_CHUNK, nh)
    g = jnp.stack([jnp.cumsum(g[:, 0], axis=2),
                   jnp.flip(jnp.cumsum(jnp.flip(g[:, 1], axis=2), axis=2), axis=2)], axis=1)
    gcol = g.reshape(bsz, 2, length, nh)
    grow = g.transpose(0, 1, 4, 2, 3)[:, :, :, :, None, :]
    o_f, o_b, s_out = gdn_scan(qkv, gcol, beta, grow, s0, nh)
    return _proj(gdn_post(o_f, o_b, proj, p['gdn_norm_w']), w_out, out_dtype=BF16), s_out


FFN_PAD = 1024


def _ffn_weights(p):
    f = p['ffn_b_dw'].shape[0]
    fp = -(-f // FFN_PAD) * FFN_PAD if f > FFN_PAD else f
    pad = lambda a, axis: jnp.pad(a, [(0, fp - f) if ax == axis else (0, 0) for ax in range(a.ndim)])
    w_up = p['ffn_w_up'].astype(BF16)
    w_up = jnp.concatenate([pad(w_up[:, :f], 1), pad(w_up[:, f:], 1)], axis=1)
    return w_up, pad(p['ffn_w_down'].astype(BF16), 0), pad(p['ffn_w_dw'], 2), pad(p['ffn_b_dw'], 0)


def _conv_glu(h, ffn_w, rows, cols):
    w_up, w_down, w_dw, b_dw = ffn_w
    assert h.shape[1] == rows * cols
    return _proj(ffn_act(_proj(h, w_up, out_dtype=BF16), w_dw, b_dw, cols), w_down, out_dtype=BF16)


def kernel(x, c, ctx, c_ctx, l0_w_ada, l0_b_ada, l0_ln1_g, l0_ln1_b, l0_ln2_g, l0_ln2_b, l0_hy_w_in, l0_hy_b_in, l0_hy_w_short, l0_hy_b_short, l0_hy_f_w1, l0_hy_f_b1, l0_hy_f_w2, l0_hy_f_b2, l0_hy_f_w3, l0_hy_f_b3, l0_hy_f_wout, l0_hy_f_freq, l0_hy_skip, l0_hy_w_out, l0_hy_b_out, l0_ffn_w_up, l0_ffn_w_dw, l0_ffn_b_dw, l0_ffn_w_down, l1_w_ada, l1_b_ada, l1_ln1_g, l1_ln1_b, l1_ln2_g, l1_ln2_b, l1_gdn_w_in, l1_gdn_w_conv, l1_gdn_a_log, l1_gdn_dt_bias, l1_gdn_norm_w, l1_gdn_w_out, l1_ffn_w_up, l1_ffn_w_dw, l1_ffn_b_dw, l1_ffn_w_down):
    layers = (
        dict(w_ada=l0_w_ada, b_ada=l0_b_ada, ln1_g=l0_ln1_g, ln1_b=l0_ln1_b, ln2_g=l0_ln2_g, ln2_b=l0_ln2_b,
             hy_w_in=l0_hy_w_in, hy_b_in=l0_hy_b_in, hy_w_short=l0_hy_w_short, hy_b_short=l0_hy_b_short,
             hy_f_w1=l0_hy_f_w1, hy_f_b1=l0_hy_f_b1, hy_f_w2=l0_hy_f_w2, hy_f_b2=l0_hy_f_b2,
             hy_f_w3=l0_hy_f_w3, hy_f_b3=l0_hy_f_b3, hy_f_wout=l0_hy_f_wout, hy_f_freq=l0_hy_f_freq,
             hy_skip=l0_hy_skip, hy_w_out=l0_hy_w_out, hy_b_out=l0_hy_b_out,
             ffn_w_up=l0_ffn_w_up, ffn_w_dw=l0_ffn_w_dw, ffn_b_dw=l0_ffn_b_dw, ffn_w_down=l0_ffn_w_down),
        dict(w_ada=l1_w_ada, b_ada=l1_b_ada, ln1_g=l1_ln1_g, ln1_b=l1_ln1_b, ln2_g=l1_ln2_g, ln2_b=l1_ln2_b,
             gdn_w_in=l1_gdn_w_in, gdn_w_conv=l1_gdn_w_conv, gdn_a_log=l1_gdn_a_log,
             gdn_dt_bias=l1_gdn_dt_bias, gdn_norm_w=l1_gdn_norm_w, gdn_w_out=l1_gdn_w_out,
             ffn_w_up=l1_ffn_w_up, ffn_w_dw=l1_ffn_w_dw, ffn_b_dw=l1_ffn_b_dw, ffn_w_down=l1_ffn_w_down),
    )
    bsz, seq, d = x.shape
    rows = seq // GRID_W
    ctx_len = ctx.shape[1]
    cond = jnp.concatenate([c, c_ctx[None], jnp.zeros((8 - bsz - 1, d), F32)], axis=0)
    mods = []
    for p in layers:
        mod = matmul(jax.nn.silu(cond), p['w_ada'], p['b_ada'], tn=512)
        mods.append((jnp.split(mod[:bsz, None, :], 6, axis=-1),
                     jnp.split(mod[bsz:bsz + 1], 6, axis=-1)))
    h_lat = modulate(x, mods[0][0][0], mods[0][0][1])
    h_ctx = modulate(ctx, mods[0][1][0], mods[0][1][1])
    for i, p in enumerate(layers):
        last = i == DEPTH - 1
        (sh1, sc1, gt1, sh2, sc2, gt2), (csh1, csc1, cgt1, csh2, csc2, cgt2) = mods[i]
        nxt, cnxt = (None, None) if last else (mods[i + 1][0][:2], mods[i + 1][1][:2])
        if 'hy_w_in' in p:
            w_in, w_out = p['hy_w_in'].astype(BF16), p['hy_w_out'].astype(BF16)
            y_lat = _hyena_mixer(h_lat, p, w_in, w_out)
            y_ctx = None if last else _hyena_mixer(h_ctx, p, w_in, w_out)
        else:
            nh = p['gdn_a_log'].shape[1]
            w_qkvz = p['gdn_w_in'][:, :4 * d].astype(BF16)
            w_ab = p['gdn_w_in'][:, 4 * d:].astype(BF16)
            w_out = p['gdn_w_out'].astype(BF16)
            s0 = jnp.zeros((bsz, 2, nh, d // nh, d // nh), F32)
            y_ctx, s_ctx = _gdn_mixer(h_ctx, p, w_qkvz, w_ab, w_out, s0)
            y_lat, _ = _gdn_mixer(h_lat, p, w_qkvz, w_ab, w_out, s_ctx)
        ffn_w = _ffn_weights(p)
        x, h_mid = post_norm(x, y_lat, gt1, p['ln1_g'], p['ln1_b'], sh2, sc2)
        ffn = _conv_glu(h_mid, ffn_w, rows, GRID_W)
        x, h_lat = post_norm(x, ffn, gt2, p['ln2_g'], p['ln2_b'], *(nxt or ()))
        if not last:
            ctx, h_mid = post_norm(ctx, y_ctx, cgt1, p['ln1_g'], p['ln1_b'], csh2, csc2)
            ffn = _conv_glu(h_mid, ffn_w, 1, ctx_len)
            ctx, h_ctx = post_norm(ctx, ffn, cgt2, p['ln2_g'], p['ln2_b'], *cnxt)
    return x
```

```python
import functools
import math

import numpy as np
import jax
import jax.numpy as jnp
from jax import lax
from jax.experimental import pallas as pl
from jax.experimental.pallas import tpu as pltpu

F32 = jnp.float32
BF16 = jnp.bfloat16

GRID_W = 64
HY_ORDER = 2
HY_EMB_DIM = 33
HY_DECAY_TARGET = 1e-2
HY_FAST_DECAY_PCT = 0.3
HY_SLOW_DECAY_PCT = 1.5
GDN_CHUNK = 128
LN_EPS = 1e-5
RMS_EPS = 1e-6
L2_EPS = 1e-6
DEPTH = 2
DN_ALPHA = (2 * DEPTH) ** 0.25

LANES = 128
SLAB = 128
SLAB_PITCH = SLAB + 8
VMEM_LIMIT = 56 * 1024 * 1024


def _cparams(sem):
    return pltpu.CompilerParams(dimension_semantics=sem, vmem_limit_bytes=VMEM_LIMIT)


def _mm_kernel(a_ref, w_ref, b_ref, o_ref, acc_ref, *, nk):
    part = jnp.dot(a_ref[...].astype(BF16), w_ref[...].astype(BF16), preferred_element_type=F32)
    if nk == 1:
        o_ref[...] = (part + b_ref[...]).astype(o_ref.dtype)
        return
    k = pl.program_id(2)

    @pl.when(k == 0)
    def _():
        acc_ref[...] = part

    @pl.when(k > 0)
    def _():
        acc_ref[...] += part

    @pl.when(k == nk - 1)
    def _():
        o_ref[...] = (acc_ref[...] + b_ref[...]).astype(o_ref.dtype)


def _pick(n, pref):
    if n <= pref:
        return n
    t = (pref // LANES) * LANES
    while t >= LANES:
        if n % t == 0:
            return t
        t -= LANES
    return n


def matmul(a, w, bias=None, out_dtype=F32, tm=1024, tn=1024, tk=4096):
    m, kd = a.shape
    kd2, n = w.shape
    assert kd == kd2
    tm, tn, tk = _pick(m, tm), _pick(n, tn), _pick(kd, tk)
    nk = kd // tk
    if bias is None:
        bias = jnp.zeros((n,), F32)
    bias = bias.reshape(1, n).astype(F32)
    return pl.pallas_call(
        functools.partial(_mm_kernel, nk=nk),
        grid=(m // tm, n // tn, nk),
        in_specs=[pl.BlockSpec((tm, tk), lambda i, j, k: (i, k)),
                  pl.BlockSpec((tk, tn), lambda i, j, k: (k, j)),
                  pl.BlockSpec((1, tn), lambda i, j, k: (0, j))],
        out_specs=pl.BlockSpec((tm, tn), lambda i, j, k: (i, j)),
        out_shape=jax.ShapeDtypeStruct((m, n), out_dtype),
        scratch_shapes=[pltpu.VMEM((tm, tn) if nk > 1 else (8, LANES), F32)],
        compiler_params=_cparams(("parallel", "parallel", "arbitrary")),
        name="matmul",
    )(a, w, bias)


def _real_form(mat):
    return np.block([[mat.real, -mat.imag], [mat.imag, mat.real]])


@functools.lru_cache(maxsize=None)
def _two_stage_consts(n1, n2):
    n = n1 * n2
    h1 = n1 // 2
    a = np.arange(n1)
    b = np.arange(n2)
    ma = np.exp(-2j * np.pi * np.outer(a, a) / n1)
    mb = np.exp(-2j * np.pi * np.outer(b, b) / n2)
    fa_half = _real_form(ma[:, :h1])
    fa_real = np.concatenate([ma.real, ma.imag], axis=0)
    fb = _real_form(mb)
    fbi = _real_form(np.conj(mb))
    fai = _real_form(np.conj(ma)[:h1, :] / n)
    tw = np.exp(-2j * np.pi * b / n)
    tw = np.stack([np.broadcast_to(tw.real[:, None], (n2, LANES)),
                   np.broadcast_to(tw.imag[:, None], (n2, LANES))])
    return dict(fa_half=fa_half, fa_real=fa_real, fb=fb, fbi=fbi, fai=fai, tw=tw)


@functools.lru_cache(maxsize=None)
def _dense_consts(length):
    n = 2 * length
    f = np.arange(n)
    t = np.arange(length)
    fwd = np.exp(-2j * np.pi * np.outer(f, t) / n)
    fwd_full = np.exp(-2j * np.pi * np.outer(f, f) / n)
    inv = np.exp(2j * np.pi * np.outer(t, f) / n) / n
    return dict(fwd=_real_form(fwd),
                fwd_real=np.concatenate([fwd_full.real, fwd_full.imag], axis=0),
                inv=_real_form(inv))


def _cmul(ar, ai, br, bi):
    return ar * br - ai * bi, ar * bi + ai * br


def _twiddle_init(t_ref):
    t_ref[0] = jnp.ones(t_ref.shape[1:], F32)
    t_ref[1] = jnp.zeros(t_ref.shape[1:], F32)


def _twiddle_step(t_ref, tw_ref, tr, ti):
    nr, ni = _cmul(tr, ti, tw_ref[0], tw_ref[1])
    t_ref[0] = nr
    t_ref[1] = ni


def _twiddle_pair(t_ref, tw_ref):
    tr, ti = t_ref[0], t_ref[1]
    return (tr, ti), _cmul(tr, ti, tw_ref[0], tw_ref[1])


def _column_stage(mat, bufr, bufi, rows_in, rows_out, complex_in):
    def column(b):
        xr = bufr[pl.ds(b, rows_in, stride=SLAB_PITCH), :]
        if not complex_in:
            return xr
        return jnp.concatenate([xr, bufi[pl.ds(b, rows_in, stride=SLAB_PITCH), :]], axis=0)

    def body(p, c):
        x = jnp.concatenate([column(2 * p), column(2 * p + 1)], axis=1).astype(BF16)
        y = jnp.dot(mat, x, preferred_element_type=F32)
        for q in range(2):
            yq = y[:, q * LANES:(q + 1) * LANES]
            bufr[pl.ds(2 * p + q, rows_out, stride=SLAB_PITCH), :] = yq[:rows_out]
            bufi[pl.ds(2 * p + q, rows_out, stride=SLAB_PITCH), :] = yq[rows_out:]
        return c
    lax.fori_loop(0, SLAB // 2, body, 0, unroll=8)


def _filter_spectrum_kernel(feat_ref, featb_ref, wf_ref, wb_ref, delta_ref, fa_ref, fb_ref, tw_ref, o_ref,
                            bufr, bufi, t_ref, *, n1, sb):
    j = pl.program_id(1)
    pitch = SLAB_PITCH
    h1 = n1 // 2
    length = h1 * SLAB

    @pl.when(j == 0)
    def _():
        rate = delta_ref[...] * (1.0 / (length - 1))
        pos = lax.broadcasted_iota(jnp.int32, (SLAB, LANES), 0).astype(F32)
        wf = wf_ref[...].astype(BF16)
        wb = wb_ref[...].astype(BF16)

        def load(a, c):
            src = pl.multiple_of(a * SLAB, SLAB)
            m = pos + lax.convert_element_type(a * SLAB, F32)
            hf = jnp.dot(feat_ref[pl.ds(src, SLAB), :].astype(BF16), wf, preferred_element_type=F32)
            hb = jnp.dot(featb_ref[pl.ds(src, SLAB), :].astype(BF16), wb, preferred_element_type=F32)
            bufr[pl.ds(pl.multiple_of(a * pitch, 8), SLAB), :] = hf * jnp.exp(-m * rate)
            bufr[pl.ds(pl.multiple_of((h1 + a) * pitch, 8), SLAB), :] = hb * jnp.exp((m - length) * rate)
            return c
        lax.fori_loop(0, h1, load, 0, unroll=4)
        tap0 =jnp.dot(feat_ref[0:8, :].astype(BF16), wb, preferred_element_type=F32)
        bufr[0:1, :] = bufr[0:1, :] + tap0[0:1]
        _column_stage(fa_ref[...], bufr, bufi, n1, n1, False)
        _twiddle_init(t_ref)

    fb = fb_ref[...]

    def stage_b(s, c):
        twiddles = _twiddle_pair(t_ref, tw_ref)
        cols = []
        for q in range(2):
            row = pl.multiple_of((j * sb + 2 * s + q) * pitch, 8)
            yr, yi = _cmul(bufr[pl.ds(row, SLAB), :], bufi[pl.ds(row, SLAB), :], *twiddles[q])
            cols.append(jnp.concatenate([yr, yi], axis=0))
        spec = jnp.dot(fb, jnp.concatenate(cols, axis=1).astype(BF16), preferred_element_type=F32)
        for q in range(2):
            out = pl.multiple_of((2 * s + q) * SLAB, SLAB)
            o_ref[0, pl.ds(out, SLAB), :] = spec[:SLAB, q * LANES:(q + 1) * LANES]
            o_ref[1, pl.ds(out, SLAB), :] = spec[SLAB:, q * LANES:(q + 1) * LANES]
        _twiddle_step(t_ref, tw_ref, *twiddles[1])
        return c
    lax.fori_loop(0, sb // 2, stage_b, 0, unroll=4)


def _slab_blocks(n1):
    return 8 if n1 % 8 == 0 and n1 >= 16 else 1


def filter_spectrum(feat, feat_circ, w_out, fwd_group, bwd_group, deltas):
    length, fh = feat.shape
    d = deltas.shape[0]
    n = 2 * length
    n1 = n // SLAB
    kb = _slab_blocks(n1)
    sb = n1 // kb
    c = _two_stage_consts(n1, SLAB)
    fa = jnp.asarray(c["fa_real"], BF16)
    fb = jnp.asarray(c["fb"], BF16)
    tw = jnp.asarray(c["tw"], F32)
    const = lambda shape: pl.BlockSpec(shape, lambda i, j: (0,) * len(shape))
    return pl.pallas_call(
        functools.partial(_filter_spectrum_kernel, n1=n1, sb=sb),
        grid=(d // LANES, kb),
        in_specs=[const(feat.shape), const(feat.shape),
                  pl.BlockSpec((fh, LANES), lambda i, j: (0, fwd_group * (d // LANES) + i)),
                  pl.BlockSpec((fh, LANES), lambda i, j: (0, bwd_group * (d // LANES) + i)),
                  pl.BlockSpec((1, LANES), lambda i, j: (0, i)),
                  const(fa.shape), const(fb.shape), const(tw.shape)],
        out_specs=pl.BlockSpec((2, sb * SLAB, LANES), lambda i, j: (0, j, i)),
        out_shape=jax.ShapeDtypeStruct((2, n, d), F32),
        scratch_shapes=[pltpu.VMEM((n1 * SLAB_PITCH, LANES), F32),
                        pltpu.VMEM((n1 * SLAB_PITCH, LANES), F32),
                        pltpu.VMEM((2, SLAB, LANES), F32)],
        compiler_params=_cparams(("parallel", "arbitrary")),
        name="filter_spectrum",
    )(feat, feat_circ, w_out, w_out, deltas.reshape(1, d), fa, fb, tw)


def _fftconv_kernel(z_ref, g_ref, ks_ref, skip_ref, fa_ref, fb_ref, fbi_ref, fai_ref, tw_ref, o_ref,
                    bufr, bufi, t_ref, *, n1, sb, kb):
    j = pl.program_id(1)
    pitch = SLAB_PITCH
    h1 = n1 // 2

    @pl.when(j == 0)
    def _():
        def load(a, c):
            src = pl.multiple_of(a * SLAB, SLAB)
            dst = pl.multiple_of(a * pitch, 8)
            bufr[pl.ds(dst, SLAB), :] = z_ref[0, pl.ds(src, SLAB), :].astype(F32)
            bufi[pl.ds(dst, SLAB), :] = z_ref[1, pl.ds(src, SLAB), :].astype(F32)
            return c
        lax.fori_loop(0, h1, load, 0)
        _column_stage(fa_ref[...], bufr, bufi, h1, n1, True)
        _twiddle_init(t_ref)

    fb = fb_ref[...]
    fbi = fbi_ref[...]

    def stage_b(s, c):
        twiddles = _twiddle_pair(t_ref, tw_ref)
        rows = [pl.multiple_of((j * sb + 2 * s + q) * pitch, 8) for q in range(2)]
        cols = []
        for q in range(2):
            yr, yi = _cmul(bufr[pl.ds(rows[q], SLAB), :], bufi[pl.ds(rows[q], SLAB), :], *twiddles[q])
            cols.append(jnp.concatenate([yr, yi], axis=0))
        spec = jnp.dot(fb, jnp.concatenate(cols, axis=1).astype(BF16), preferred_element_type=F32)
        prods = []
        for q in range(2):
            krow = pl.multiple_of((2 * s + q) * SLAB, SLAB)
            sq = spec[:, q * LANES:(q + 1) * LANES]
            pr, pi = _cmul(sq[:SLAB], sq[SLAB:], ks_ref[0, pl.ds(krow, SLAB), :], ks_ref[1, pl.ds(krow, SLAB), :])
            prods.append(jnp.concatenate([pr, pi], axis=0))
        back = jnp.dot(fbi, jnp.concatenate(prods, axis=1).astype(BF16), preferred_element_type=F32)
        for q in range(2):
            bq = back[:, q * LANES:(q + 1) * LANES]
            ur, ui = _cmul(bq[:SLAB], bq[SLAB:], twiddles[q][0], -twiddles[q][1])
            bufr[pl.ds(rows[q], SLAB), :] = ur
            bufi[pl.ds(rows[q], SLAB), :] = ui
        _twiddle_step(t_ref, tw_ref, *twiddles[1])
        return c
    lax.fori_loop(0, sb // 2, stage_b, 0, unroll=4)

    @pl.when(j == kb - 1)
    def _():
        _column_stage(fai_ref[...], bufr, bufi, n1, h1, True)
        skip = skip_ref[...]

        def store(a, c):
            dst = pl.multiple_of(a * SLAB, SLAB)
            src = pl.multiple_of(a * pitch, 8)
            for bi, buf in enumerate((bufr, bufi)):
                zz = z_ref[bi, pl.ds(dst, SLAB), :].astype(F32)
                gg = g_ref[bi, pl.ds(dst, SLAB), :].astype(F32)
                o_ref[bi, pl.ds(dst, SLAB), :] = (gg * (buf[pl.ds(src, SLAB), :] + zz * skip)).astype(o_ref.dtype)
            return c
        lax.fori_loop(0, h1, store, 0)


def fftconv_gated(z, gate, kspec, skip, z_col=0, gate_col=0, out_dtype=BF16):
    bsz, length, _ = z.shape
    d = skip.shape[0]
    assert bsz == 2
    n1 = 2 * length // SLAB
    kb = _slab_blocks(n1)
    sb = n1 // kb
    c = _two_stage_consts(n1, SLAB)
    fa = jnp.asarray(c["fa_half"], BF16)
    fb = jnp.asarray(c["fb"], BF16)
    fbi = jnp.asarray(c["fbi"], BF16)
    fai = jnp.asarray(c["fai"], BF16)
    tw = jnp.asarray(c["tw"], F32)
    const = lambda shape: pl.BlockSpec(shape, lambda i, j: (0,) * len(shape))
    seq = lambda col=0: pl.BlockSpec((2, length, LANES), lambda i, j: (0, 0, col // LANES + i))
    return pl.pallas_call(
        functools.partial(_fftconv_kernel, n1=n1, sb=sb, kb=kb),
        grid=(d // LANES, kb),
        in_specs=[seq(z_col), seq(gate_col),
                  pl.BlockSpec((2, sb * SLAB, LANES), lambda i, j: (0, j, i)),
                  pl.BlockSpec((1, LANES), lambda i, j: (0, i)),
                  const(fa.shape), const(fb.shape), const(fbi.shape), const(fai.shape), const(tw.shape)],
        out_specs=seq(),
        out_shape=jax.ShapeDtypeStruct((2, length, d), out_dtype),
        scratch_shapes=[pltpu.VMEM((n1 * SLAB_PITCH, LANES), F32),
                        pltpu.VMEM((n1 * SLAB_PITCH, LANES), F32),
                        pltpu.VMEM((2, SLAB, LANES), F32)],
        compiler_params=_cparams(("parallel", "arbitrary")),
        name="fftconv",
    )(z, gate, kspec, skip.reshape(1, d), fa, fb, fbi, fai, tw)


def _dense_conv_kernel(z_ref, g_ref, k_ref, skip_ref, fwd_ref, fwdk_ref, inv_ref, o_ref, *, length):
    n = 2 * length
    x = jnp.concatenate([z_ref[0], z_ref[1]], axis=0)
    spec = jnp.dot(fwd_ref[...], x.astype(BF16), preferred_element_type=F32)
    kspec = jnp.dot(fwdk_ref[...], k_ref[...].astype(BF16), preferred_element_type=F32)
    pr, pi = _cmul(spec[:n], spec[n:], kspec[:n], kspec[n:])
    y = jnp.dot(inv_ref[...], jnp.concatenate([pr, pi], axis=0).astype(BF16), preferred_element_type=F32)
    skip = skip_ref[...]
    for bi in range(2):
        yy = y[bi * length:(bi + 1) * length]
        o_ref[bi] = (g_ref[bi].astype(F32) * (yy + z_ref[bi].astype(F32) * skip)).astype(o_ref.dtype)


def dense_conv_gated(z, gate, k_time, skip, z_col=0, gate_col=0, out_dtype=BF16):
    bsz, length, _ = z.shape
    d = skip.shape[0]
    assert bsz == 2
    c = _dense_consts(length)
    fwd = jnp.asarray(c["fwd"], BF16)
    fwdk = jnp.asarray(c["fwd_real"], BF16)
    inv = jnp.asarray(c["inv"], BF16)
    tile = 2 * LANES if d % (2 * LANES) == 0 else LANES
    const = lambda shape: pl.BlockSpec(shape, lambda i: (0,) * len(shape))
    seq = lambda col=0: pl.BlockSpec((2, length, tile), lambda i: (0, 0, col // tile + i))
    return pl.pallas_call(
        functools.partial(_dense_conv_kernel, length=length),
        grid=(d // tile,),
        in_specs=[seq(z_col), seq(gate_col), pl.BlockSpec((2 * length, tile), lambda i: (0, i)),
                  pl.BlockSpec((1, tile), lambda i: (0, i)),
                  const(fwd.shape), const(fwdk.shape), const(inv.shape)],
        out_specs=seq(),
        out_shape=jax.ShapeDtypeStruct((2, length, d), out_dtype),
        compiler_params=_cparams(("parallel",)),
        name="dense_conv",
    )(z, gate, k_time, skip.reshape(1, d), fwd, fwdk, inv)


def _bdot(a, b):
    return lax.dot_general(a.astype(BF16), b.astype(BF16), (((2,), (1,)), ((0,), (0,))),
                           preferred_element_type=F32)


def _bdot_nt(a, b):
    return lax.dot_general(a.astype(BF16), b.astype(BF16), (((2,), (2,)), ((0,), (0,))),
                           preferred_element_type=F32)


def _dot(a, b):
    return jnp.dot(a.astype(BF16), b.astype(BF16), preferred_element_type=F32)


def _dot_tn(a, b):
    return lax.dot_general(a.astype(BF16), b.astype(BF16), (((0,), (0,)), ((), ())), preferred_element_type=F32)


def _gdn_kernel(qf_ref, kf_ref, vf_ref, qb_ref, kb_ref, vb_ref, gcf_ref, bcf_ref, gcb_ref, bcb_ref,
                grf_ref, grb_ref, s0_ref, of_ref, ob_ref, sout_ref,
                s_ref, u_ref, w_ref, qg_ref, at_ref, gl_ref, mw_ref, n_ref, sc_ref, *, r, nb, hb, nh):
    cs = GDN_CHUNK
    head0 = pl.program_id(1) * hb
    j = pl.program_id(2)

    @pl.when(j == 0)
    def _():
        s_ref[...] = s0_ref[0]

    ii = lax.broadcasted_iota(jnp.int32, (cs, cs), 0)
    jj = lax.broadcasted_iota(jnp.int32, (cs, cs), 1)
    eye = (ii == jj).astype(F32)
    blk_same = [jnp.right_shift(ii, lvl) == jnp.right_shift(jj, lvl) for lvl in range(int(math.log2(cs)) + 1)]
    lane_head = lax.broadcasted_iota(jnp.int32, (r * cs, nh), 1)

    for dr in range(2):
        q_ref, k_ref, v_ref = (qf_ref, kf_ref, vf_ref) if dr == 0 else (qb_ref, kb_ref, vb_ref)
        gcol_ref, bcol_ref = (gcf_ref, bcf_ref) if dr == 0 else (gcb_ref, bcb_ref)
        grow_ref = grf_ref if dr == 0 else grb_ref
        incl = (ii >= jj) if dr == 0 else (ii <= jj)
        strict = (ii > jj) if dr == 0 else (ii < jj)
        last = cs - 1 if dr == 0 else 0
        def heads(ref):
            return jnp.concatenate([ref[0, :, hh * LANES:(hh + 1) * LANES].astype(F32).reshape(r, cs, LANES)
                                    for hh in range(hb)], axis=0)

        def head_cols(ref):
            return jnp.concatenate(
                [jnp.sum(jnp.where(lane_head == head0 + hh, ref[0, 0], 0.0), axis=1, keepdims=True).reshape(r, cs, 1)
                 for hh in range(hb)], axis=0)
        q, k, v = heads(q_ref), heads(k_ref), heads(v_ref)
        gc, bc = head_cols(gcol_ref), head_cols(bcol_ref)
        gr = grow_ref[0, 0].reshape(hb * r, 1, cs)
        g_last = gr[:, :, last:last + 1]
        decay = jnp.where(incl, jnp.exp(jnp.where(incl, gc - gr, 0.0)), 0.0)
        kbeta = k * bc
        a_mat = jnp.where(strict, _bdot_nt(kbeta, k) * decay, 0.0).astype(BF16).astype(F32)
        t = eye - jnp.where(blk_same[1] & ~blk_same[0], a_mat, 0.0)
        for lvl in range(1, len(blk_same) - 1):
            x = jnp.where(blk_same[lvl + 1] & ~blk_same[lvl], a_mat, 0.0)
            t = t - _bdot(t, _bdot(x, t))
        eg = jnp.exp(gc)
        sol = _bdot(t, jnp.concatenate([v * bc, kbeta * eg], axis=2))
        u, w = sol[:, :, :LANES], sol[:, :, LANES:]
        per_head = lambda a: a.reshape((hb, r) + a.shape[1:])
        u_ref[dr] = per_head(u)
        w_ref[dr] = per_head(w)
        at_ref[dr] = per_head(jnp.where(incl, _bdot_nt(q, k) * decay, 0.0))
        qg_ref[dr] = per_head(q * eg)
        kg_t = jnp.swapaxes(k * jnp.exp(g_last - gc), 1, 2)
        mw_ref[dr] = per_head(-_bdot(kg_t, w))
        n_ref[dr] = per_head(_bdot(kg_t, u))
        gl_ref[dr] = per_head(jnp.broadcast_to(jnp.exp(g_last), (hb * r, 8, LANES)))

    def scan(p, carry):
        for dr in range(2):
            c = p if dr == 0 else r - 1 - p
            for hh in range(hb):
                s = s_ref[dr, hh]
                sc_ref[dr, hh, c] = s
                s_ref[dr, hh] = s * gl_ref[dr, hh, c][0:1, :] + _dot(mw_ref[dr, hh, c], s) + n_ref[dr, hh, c]
        return carry
    lax.fori_loop(0, r, scan, 0, unroll=2)

    for dr in range(2):
        o_ref = of_ref if dr == 0 else ob_ref
        flat = lambda ref: ref[dr].reshape((hb * r,) + ref.shape[3:])
        sc = flat(sc_ref)
        v_new = flat(u_ref) - _bdot(flat(w_ref), sc)
        o = _bdot(flat(qg_ref), sc) + _bdot(flat(at_ref), v_new)
        for hh in range(hb):
            o_ref[0, :, hh * LANES:(hh + 1) * LANES] = (
                o[hh * r:(hh + 1) * r].reshape(r * cs, LANES).astype(o_ref.dtype))

    @pl.when(j == nb - 1)
    def _():
        sout_ref[0] = s_ref[...]


def gdn_scan(qkv, gcol, bcol, grow, s0, nh):
    bsz, length, d3 = qkv.shape
    d = d3 // 3
    dh = d // nh
    assert dh == LANES
    hb = 4 if nh % 4 == 0 else 2 if nh % 2 == 0 else 1
    ng = nh // hb
    nch = length // GDN_CHUNK
    r = 2 * ROW_TILE // GDN_CHUNK
    r = r if nch % r == 0 else nch
    nb = nch // r
    rows = r * GDN_CHUNK
    blk = lambda j, rev: nb - 1 - j if rev else j
    seq = lambda off, rev: pl.BlockSpec((1, rows, hb * LANES), lambda b, g, j: (b, blk(j, rev), off * ng + g))
    col = lambda rev: pl.BlockSpec((1, 1, rows, nh), lambda b, g, j: (b, rev, blk(j, rev), 0))
    row = lambda rev: pl.BlockSpec((1, 1, hb, r, 1, GDN_CHUNK), lambda b, g, j: (b, rev, g, blk(j, rev), 0, 0))
    out = lambda rev: pl.BlockSpec((1, rows, hb * LANES), lambda b, g, j: (b, blk(j, rev), g))
    state = pl.BlockSpec((1, 2, hb, dh, dh), lambda b, g, j: (b, 0, g, 0, 0))
    per_chunk = lambda *shape: pltpu.VMEM((2, hb, r) + shape, F32)
    return pl.pallas_call(
        functools.partial(_gdn_kernel, r=r, nb=nb, hb=hb, nh=nh),
        grid=(bsz, ng, nb),
        in_specs=[seq(0, 0), seq(1, 0), seq(2, 0), seq(0, 1), seq(1, 1), seq(2, 1),
                  col(0), col(0), col(1), col(1), row(0), row(1), state],
        out_specs=[out(0), out(1), state],
        out_shape=[jax.ShapeDtypeStruct((bsz, length, d), BF16),
                   jax.ShapeDtypeStruct((bsz, length, d), BF16),
                   jax.ShapeDtypeStruct((bsz, 2, nh, dh, dh), F32)],
        scratch_shapes=[pltpu.VMEM((2, hb, dh, dh), F32),
                        per_chunk(GDN_CHUNK, LANES), per_chunk(GDN_CHUNK, LANES), per_chunk(GDN_CHUNK, LANES),
                        per_chunk(GDN_CHUNK, GDN_CHUNK), per_chunk(8, LANES),
                        per_chunk(dh, dh), per_chunk(dh, dh), per_chunk(dh, dh)],
        compiler_params=_cparams(("parallel", "parallel", "arbitrary")),
        name="gdn_scan",
    )(qkv, qkv, qkv, qkv, qkv, qkv, gcol, bcol, gcol, bcol, grow, grow, s0)


ROW_TILE = 256
HALO = 16


def _per_batch(v, bsz):
    return jnp.broadcast_to(v.reshape(-1, 1, v.shape[-1]), (bsz, 1, v.shape[-1]))


def _modulate_kernel(x_ref, sh_ref, sc_ref, h_ref):
    h_ref[0] = (x_ref[0] * (1.0 + sc_ref[0]) + sh_ref[0]).astype(h_ref.dtype)


def modulate(x, shift, scale):
    bsz, length, d = x.shape
    tr = _pick(length, ROW_TILE)
    row = pl.BlockSpec((1, tr, d), lambda b, i: (b, i, 0))
    vec = pl.BlockSpec((1, 1, d), lambda b, i: (b, 0, 0))
    return pl.pallas_call(
        _modulate_kernel, grid=(bsz, length // tr), in_specs=[row, vec, vec], out_specs=row,
        out_shape=jax.ShapeDtypeStruct(x.shape, BF16), compiler_params=_cparams(("parallel", "parallel")),
        name="modulate",
    )(x, _per_batch(shift, bsz), _per_batch(scale, bsz))


def _post_norm_kernel(x_ref, y_ref, gate_ref, g_ref, b_ref, sh_ref, sc_ref, xo_ref, *h_ref):
    v = DN_ALPHA * x_ref[0] + gate_ref[0] * y_ref[0].astype(F32)
    cen = v - jnp.mean(v, axis=1, keepdims=True)
    out = cen * lax.rsqrt(jnp.mean(cen * cen, axis=1, keepdims=True) + LN_EPS) * g_ref[...] + b_ref[...]
    xo_ref[0] = out
    if h_ref:
        h_ref[0][0] = (out * (1.0 + sc_ref[0]) + sh_ref[0]).astype(BF16)


def post_norm(x, y, gate, g, b, shift=None, scale=None):
    bsz, length, d = x.shape
    with_h = shift is not None
    if not with_h:
        shift = scale = jnp.zeros((1, d), F32)
    tr = _pick(length, ROW_TILE)
    row = pl.BlockSpec((1, tr, d), lambda b, i: (b, i, 0))
    vec = pl.BlockSpec((1, 1, d), lambda b, i: (b, 0, 0))
    par = pl.BlockSpec((1, d), lambda b, i: (0, 0))
    out = pl.pallas_call(
        _post_norm_kernel, grid=(bsz, length // tr),
        in_specs=[row, row, vec, par, par, vec, vec],
        out_specs=[row, row] if with_h else [row],
        out_shape=[jax.ShapeDtypeStruct(x.shape, F32)] + ([jax.ShapeDtypeStruct(x.shape, BF16)] if with_h else []),
        compiler_params=_cparams(("parallel", "parallel")), name="post_norm",
    )(x, y, _per_batch(gate, bsz), g.reshape(1, d), b.reshape(1, d), _per_batch(shift, bsz), _per_batch(scale, bsz))
    return out if with_h else (out[0], None)


def _seq_neighbours(x, prev_ref, next_ref, n_steps):
    i = pl.program_id(2)
    rows = x.shape[0]
    row = lax.broadcasted_iota(jnp.int32, x.shape, 0)
    before = jnp.where(i == 0, 0.0, prev_ref[0, HALO - 1:HALO, :].astype(F32))
    after = jnp.where(i == n_steps - 1, 0.0, next_ref[0, 0:1, :].astype(F32))
    prev = jnp.where(row == 0, before, pltpu.roll(x, 1, 0))
    nxt = jnp.where(row == rows - 1, after, pltpu.roll(x, rows - 1, 0))
    return prev, nxt


def _seq_specs(tr, tc, length):
    per = tr // HALO
    last = length // HALO - 1
    main = pl.BlockSpec((1, tr, tc), lambda b, j, i: (b, i, j))
    prev = pl.BlockSpec((1, HALO, tc), lambda b, j, i: (b, jnp.maximum(i * per - 1, 0), j))
    nxt = pl.BlockSpec((1, HALO, tc), lambda b, j, i: (b, jnp.minimum((i + 1) * per, last), j))
    return main, prev, nxt


def _hy_short_conv_kernel(x_ref, p_ref, n_ref, w_ref, b_ref, o_ref, *, n_steps):
    x = x_ref[0].astype(F32)
    prev, nxt = _seq_neighbours(x, p_ref, n_ref, n_steps)
    o_ref[0] = (prev * w_ref[0:1, :] + x * w_ref[1:2, :] + nxt * w_ref[2:3, :] + b_ref[...]).astype(o_ref.dtype)


def hy_short_conv(x, w, b):
    bsz, length, ch = x.shape
    tr, tc = _pick(length, 2 * ROW_TILE), _pick(ch, 1024)
    main, prev, nxt = _seq_specs(tr, tc, length)
    return pl.pallas_call(
        functools.partial(_hy_short_conv_kernel, n_steps=length // tr),
        grid=(bsz, ch // tc, length // tr),
        in_specs=[main, prev, nxt, pl.BlockSpec((3, tc), lambda b_, j, i: (0, j)),
                  pl.BlockSpec((1, tc), lambda b_, j, i: (0, j))],
        out_specs=main, out_shape=jax.ShapeDtypeStruct(x.shape, BF16),
        compiler_params=_cparams(("parallel", "parallel", "parallel")), name="hy_short_conv",
    )(x, x, x, w, b.reshape(1, ch))


def _gdn_pre_kernel(x_ref, p_ref, n_ref, w_ref, o_ref, *, n_steps, q_blocks, qk_blocks, q_scale):
    x = x_ref[0].astype(F32)
    prev, nxt = _seq_neighbours(x, p_ref, n_ref, n_steps)
    y = prev * w_ref[0:1, :] + x * w_ref[1:2, :] + nxt * w_ref[2:3, :]
    y = y * jax.nn.sigmoid(y)
    j = pl.program_id(1)
    scale = jnp.where(j < q_blocks, q_scale, 1.0)
    for h in range(y.shape[1] // LANES):
        lanes = slice(h * LANES, (h + 1) * LANES)
        yh = y[:, lanes]
        normed = yh * (lax.rsqrt(jnp.sum(yh * yh, axis=1, keepdims=True) + L2_EPS) * scale)
        o_ref[0, :, lanes] = jnp.where(j < qk_blocks, normed, yh).astype(o_ref.dtype)


def gdn_pre(proj, w_conv, d, dh):
    bsz, length, _ = proj.shape
    tr, tc = _pick(length, 2 * ROW_TILE), _pick(d, 1024)
    main, prev, nxt = _seq_specs(tr, tc, length)
    return pl.pallas_call(
        functools.partial(_gdn_pre_kernel, n_steps=length // tr, q_blocks=d // tc, qk_blocks=2 * d // tc,
                          q_scale=dh ** -0.5),
        grid=(bsz, 3 * d // tc, length // tr),
        in_specs=[main, prev, nxt, pl.BlockSpec((3, tc), lambda b_, j, i: (0, j))],
        out_specs=main, out_shape=jax.ShapeDtypeStruct((bsz, length, 3 * d), BF16),
        compiler_params=_cparams(("parallel", "parallel", "parallel")), name="gdn_pre",
    )(proj, proj, proj, w_conv)


def _gdn_post_kernel(of_ref, ob_ref, z_ref, nw_ref, o_ref):
    o = of_ref[0].astype(F32) + ob_ref[0].astype(F32)
    z = z_ref[0].astype(F32)
    gate = z * jax.nn.sigmoid(z) * nw_ref[...]
    for h in range(o.shape[1] // LANES):
        lanes = slice(h * LANES, (h + 1) * LANES)
        oh = o[:, lanes]
        rms = lax.rsqrt(jnp.mean(oh * oh, axis=1, keepdims=True) + RMS_EPS)
        o_ref[0, :, lanes] = (oh * rms * gate[:, lanes]).astype(o_ref.dtype)


def gdn_post(o_f, o_b, proj, norm_w):
    bsz, length, d = o_f.shape
    tr, tc = _pick(length, 2 * ROW_TILE), _pick(d, 512)
    z_off = (proj.shape[2] - d) // tc
    row = pl.BlockSpec((1, tr, tc), lambda b, j, i: (b, i, j))
    nw = jnp.tile(norm_w, tc // norm_w.shape[0]).reshape(1, tc)
    return pl.pallas_call(
        _gdn_post_kernel, grid=(bsz, d // tc, length // tr),
        in_specs=[row, row, pl.BlockSpec((1, tr, tc), lambda b, j, i: (b, i, z_off + j)),
                  pl.BlockSpec((1, tc), lambda b, j, i: (0, 0))],
        out_specs=row, out_shape=jax.ShapeDtypeStruct(o_f.shape, BF16),
        compiler_params=_cparams(("parallel", "parallel", "parallel")), name="gdn_post",
    )(o_f, o_b, proj, nw)


def _ffn_act_kernel(g_ref, v_ref, up_ref, dn_ref, w_ref, b_ref, o_ref, *, n_steps, gw):
    i = pl.program_id(2)
    above = jnp.where(i == 0, 0.0, up_ref[0].astype(F32))
    below = jnp.where(i == n_steps - 1, 0.0, dn_ref[0].astype(F32))
    x = jnp.concatenate([above, g_ref[0].astype(F32), below], axis=0)
    n = x.shape[0]
    col = lax.rem(lax.broadcasted_iota(jnp.int32, x.shape, 0), gw)
    left = jnp.where(col == 0, 0.0, pltpu.roll(x, 1, 0))
    right = jnp.where(col == gw - 1, 0.0, pltpu.roll(x, n - 1, 0))
    rows = g_ref.shape[1]
    acc = b_ref[...]
    for di in range(3):
        rs = slice(di * gw, di * gw + rows)
        acc = acc + (left[rs] * w_ref[3 * di:3 * di + 1, :] + x[rs] * w_ref[3 * di + 1:3 * di + 2, :]
                     + right[rs] * w_ref[3 * di + 2:3 * di + 3, :])
    gelu = 0.5 * acc * (1.0 + lax.erf(acc * (2.0 ** -0.5)))
    o_ref[0] = (gelu * v_ref[0].astype(F32)).astype(o_ref.dtype)


def ffn_act(up, w_dw, b_dw, gw):
    bsz, length, f2 = up.shape
    f = f2 // 2
    tc = _pick(f, 1024)
    tr = gw * max(1, min(length // gw, 4 * ROW_TILE // gw))
    per = tr // gw
    last = length // gw - 1
    main = lambda off: pl.BlockSpec((1, tr, tc), lambda b, j, i: (b, i, off + j))
    halo_up = pl.BlockSpec((1, gw, tc), lambda b, j, i: (b, jnp.maximum(i * per - 1, 0), j))
    halo_dn = pl.BlockSpec((1, gw, tc), lambda b, j, i: (b, jnp.minimum((i + 1) * per, last), j))
    return pl.pallas_call(
        functools.partial(_ffn_act_kernel, n_steps=length // tr, gw=gw),
        grid=(bsz, f // tc, length // tr),
        in_specs=[main(0), main(f // tc), halo_up, halo_dn,
                  pl.BlockSpec((9, tc), lambda b, j, i: (0, j)), pl.BlockSpec((1, tc), lambda b, j, i: (0, j))],
        out_specs=main(0), out_shape=jax.ShapeDtypeStruct((bsz, length, f), BF16),
        compiler_params=_cparams(("parallel", "parallel", "parallel")), name="ffn_act",
    )(up, up, up, up, w_dw.reshape(9, f), b_dw.reshape(1, f))


def _proj(h, w, bias=None, out_dtype=F32, **tiles):
    bsz, length, kd = h.shape
    return matmul(h.reshape(bsz * length, kd).astype(BF16), w, bias, out_dtype, **tiles).reshape(bsz, length, -1)


def _hyena_filter_features(length, p, d):
    t = jnp.linspace(0.0, 1.0, length, dtype=F32)[:, None]
    bands = (HY_EMB_DIM - 1) // 2
    ang = 2.0 * math.pi * jnp.arange(length, dtype=F32)[:, None] / length
    f = jnp.linspace(1e-4, bands - 1, bands, dtype=F32)[None, :]
    z = jnp.concatenate([t, jnp.cos(f * ang), -jnp.sin(f * ang)], axis=-1)
    freq = p['hy_f_freq']
    h = jnp.sin(freq * (z @ p['hy_f_w1'] + p['hy_f_b1']))
    h = jnp.sin(freq * (h @ p['hy_f_w2'] + p['hy_f_b2']))
    h = jnp.sin(freq * (h @ p['hy_f_w3'] + p['hy_f_b3']))
    max_decay = math.log(HY_DECAY_TARGET) / HY_FAST_DECAY_PCT
    min_decay = math.log(HY_DECAY_TARGET) / HY_SLOW_DECAY_PCT
    deltas = jnp.abs(jnp.linspace(min_decay, max_decay, d, dtype=F32))
    return h, jnp.concatenate([jnp.zeros_like(h[:1]), h[:0:-1]], axis=0), deltas


def _windowed_filter(feat, feat_circ, w_fwd, w_bwd, deltas):
    length = feat.shape[0]
    m = jnp.arange(length, dtype=F32)[:, None]
    rate = deltas / (length - 1)
    head = matmul(feat, w_fwd) * jnp.exp(-m * rate)
    tail = matmul(feat_circ, w_bwd) * jnp.exp((m - length) * rate)
    return jnp.concatenate([head[:1] + matmul(feat[:8], w_bwd)[:1], head[1:], tail], axis=0)


def _hyena_mixer(h, p, w_in, w_out):
    bsz, length, d = h.shape
    u = hy_short_conv(_proj(h, w_in, p['hy_b_in'], out_dtype=BF16), p['hy_w_short'], p['hy_b_short'])
    feat, feat_circ, deltas = _hyena_filter_features(length, p, d)
    w_filt = p['hy_f_wout']
    z, z_col = u, HY_ORDER * d
    two_stage = (2 * length) % (16 * SLAB) == 0
    for o in range(HY_ORDER):
        if two_stage:
            kspec = filter_spectrum(feat, feat_circ, w_filt, 2 * o, 2 * o + 1, deltas)
            z = fftconv_gated(z, u, kspec, p['hy_skip'][o], z_col, o * d)
        else:
            k_time = _windowed_filter(feat, feat_circ, w_filt[:, 2 * o * d:(2 * o + 1) * d],
                                      w_filt[:, (2 * o + 1) * d:(2 * o + 2) * d], deltas)
            z = dense_conv_gated(z, u, k_time, p['hy_skip'][o], z_col, o * d)
        z_col = 0
    return _proj(z, w_out, p['hy_b_out'], out_dtype=BF16)


def _gdn_mixer(h, p, w_qkvz, w_ab, w_out, s0):
    bsz, length, d = h.shape
    nh = p['gdn_a_log'].shape[1]
    dh = d // nh
    proj = _proj(h, w_qkvz, out_dtype=BF16)
    ab = _proj(h, w_ab).reshape(bsz, length, 2, 2, nh)
    qkv = gdn_pre(proj, p['gdn_w_conv'], d, dh)
    g =-jnp.exp(p['gdn_a_log']) * jax.nn.softplus(ab[:, :, 0] + p['gdn_dt_bias'])
    beta = jax.nn.sigmoid(ab[:, :, 1]).transpose(0, 2, 1, 3)
    nch = length // GDN_CHUNK
    g = g.transpose(0, 2, 1, 3).reshape(bsz, 2, nch, GDN_CHUNK, nh)
    g = jnp.stack([jnp.cumsum(g[:, 0], axis=2),
                   jnp.flip(jnp.cumsum(jnp.flip(g[:, 1], axis=2), axis=2), axis=2)], axis=1)
    gcol = g.reshape(bsz, 2, length, nh)
    grow = g.transpose(0, 1, 4, 2, 3)[:, :, :, :, None, :]
    o_f, o_b, s_out = gdn_scan(qkv, gcol, beta, grow, s0, nh)
    return _proj(gdn_post(o_f, o_b, proj, p['gdn_norm_w']), w_out, out_dtype=BF16), s_out


FFN_PAD = 1024


def _ffn_weights(p):
    f = p['ffn_b_dw'].shape[0]
    fp = -(-f // FFN_PAD) * FFN_PAD if f > FFN_PAD else f
    pad = lambda a, axis: jnp.pad(a, [(0, fp - f) if ax == axis else (0, 0) for ax in range(a.ndim)])
    w_up = p['ffn_w_up'].astype(BF16)
    w_up = jnp.concatenate([pad(w_up[:, :f], 1), pad(w_up[:, f:], 1)], axis=1)
    return w_up, pad(p['ffn_w_down'].astype(BF16), 0), pad(p['ffn_w_dw'], 2), pad(p['ffn_b_dw'], 0)


def _conv_glu(h, ffn_w, rows, cols):
    w_up, w_down, w_dw, b_dw = ffn_w
    assert h.shape[1] == rows * cols
    return _proj(ffn_act(_proj(h, w_up, out_dtype=BF16), w_dw, b_dw, cols), w_down, out_dtype=BF16)


def kernel(x, c, ctx, c_ctx, l0_w_ada, l0_b_ada, l0_ln1_g, l0_ln1_b, l0_ln2_g, l0_ln2_b, l0_hy_w_in, l0_hy_b_in, l0_hy_w_short, l0_hy_b_short, l0_hy_f_w1, l0_hy_f_b1, l0_hy_f_w2, l0_hy_f_b2, l0_hy_f_w3, l0_hy_f_b3, l0_hy_f_wout, l0_hy_f_freq, l0_hy_skip, l0_hy_w_out, l0_hy_b_out, l0_ffn_w_up, l0_ffn_w_dw, l0_ffn_b_dw, l0_ffn_w_down, l1_w_ada, l1_b_ada, l1_ln1_g, l1_ln1_b, l1_ln2_g, l1_ln2_b, l1_gdn_w_in, l1_gdn_w_conv, l1_gdn_a_log, l1_gdn_dt_bias, l1_gdn_norm_w, l1_gdn_w_out, l1_ffn_w_up, l1_ffn_w_dw, l1_ffn_b_dw, l1_ffn_w_down):
    layers = (
        dict(w_ada=l0_w_ada, b_ada=l0_b_ada, ln1_g=l0_ln1_g, ln1_b=l0_ln1_b, ln2_g=l0_ln2_g, ln2_b=l0_ln2_b,
             hy_w_in=l0_hy_w_in, hy_b_in=l0_hy_b_in, hy_w_short=l0_hy_w_short, hy_b_short=l0_hy_b_short,
             hy_f_w1=l0_hy_f_w1, hy_f_b1=l0_hy_f_b1, hy_f_w2=l0_hy_f_w2, hy_f_b2=l0_hy_f_b2,
             hy_f_w3=l0_hy_f_w3, hy_f_b3=l0_hy_f_b3, hy_f_wout=l0_hy_f_wout, hy_f_freq=l0_hy_f_freq,
             hy_skip=l0_hy_skip, hy_w_out=l0_hy_w_out, hy_b_out=l0_hy_b_out,
             ffn_w_up=l0_ffn_w_up, ffn_w_dw=l0_ffn_w_dw, ffn_b_dw=l0_ffn_b_dw, ffn_w_down=l0_ffn_w_down),
        dict(w_ada=l1_w_ada, b_ada=l1_b_ada, ln1_g=l1_ln1_g, ln1_b=l1_ln1_b, ln2_g=l1_ln2_g, ln2_b=l1_ln2_b,
             gdn_w_in=l1_gdn_w_in, gdn_w_conv=l1_gdn_w_conv, gdn_a_log=l1_gdn_a_log,
             gdn_dt_bias=l1_gdn_dt_bias, gdn_norm_w=l1_gdn_norm_w, gdn_w_out=l1_gdn_w_out,
             ffn_w_up=l1_ffn_w_up, ffn_w_dw=l1_ffn_w_dw, ffn_b_dw=l1_ffn_b_dw, ffn_w_down=l1_ffn_w_down),
    )
    bsz, seq, d = x.shape
    rows = seq // GRID_W
    ctx_len = ctx.shape[1]
    cond = jnp.concatenate([c, c_ctx[None], jnp.zeros((8 - bsz - 1, d), F32)], axis=0)
    mods = []
    for p in layers:
        mod = matmul(jax.nn.silu(cond), p['w_ada'], p['b_ada'], tn=512)
        mods.append((jnp.split(mod[:bsz, None, :], 6, axis=-1),
                     jnp.split(mod[bsz:bsz + 1], 6, axis=-1)))
    h_lat = modulate(x, mods[0][0][0], mods[0][0][1])
    h_ctx = modulate(ctx, mods[0][1][0], mods[0][1][1])
    for i, p in enumerate(layers):
        last = i == DEPTH - 1
        (sh1, sc1, gt1, sh2, sc2, gt2), (csh1, csc1, cgt1, csh2, csc2, cgt2) = mods[i]
        nxt, cnxt = (None, None) if last else (mods[i + 1][0][:2], mods[i + 1][1][:2])
        if 'hy_w_in' in p:
            w_in, w_out = p['hy_w_in'].astype(BF16), p['hy_w_out'].astype(BF16)
            y_lat = _hyena_mixer(h_lat, p, w_in, w_out)
            y_ctx = None if last else _hyena_mixer(h_ctx, p, w_in, w_out)
        else:
            nh = p['gdn_a_log'].shape[1]
            w_qkvz = p['gdn_w_in'][:, :4 * d].astype(BF16)
            w_ab = p['gdn_w_in'][:, 4 * d:].astype(BF16)
            w_out = p['gdn_w_out'].astype(BF16)
            s0 = jnp.zeros((bsz, 2, nh, d // nh, d // nh), F32)
            y_ctx, s_ctx = _gdn_mixer(h_ctx, p, w_qkvz, w_ab, w_out, s0)
            y_lat, _ = _gdn_mixer(h_lat, p, w_qkvz, w_ab, w_out, s_ctx)
        ffn_w = _ffn_weights(p)
        x, h_mid = post_norm(x, y_lat, gt1, p['ln1_g'], p['ln1_b'], sh2, sc2)
        ffn = _conv_glu(h_mid, ffn_w, rows, GRID_W)
        x, h_lat = post_norm(x, ffn, gt2, p['ln2_g'], p['ln2_b'], *(nxt or ()))
        if not last:
            ctx, h_mid = post_norm(ctx, y_ctx, cgt1, p['ln1_g'], p['ln1_b'], csh2, csc2)
            ffn = _conv_glu(h_mid, ffn_w, 1, ctx_len)
            ctx, h_ctx = post_norm(ctx, ffn, cgt2, p['ln2_g'], p['ln2_b'], *cnxt)
    return x
```

```python
import functools
import math

import numpy as np
import jax
import jax.numpy as jnp
from jax import lax
from jax.experimental import pallas as pl
from jax.experimental.pallas import tpu as pltpu

F32 = jnp.float32
BF16 = jnp.bfloat16

GRID_W = 64
HY_ORDER = 2
HY_EMB_DIM = 33
HY_DECAY_TARGET = 1e-2
HY_FAST_DECAY_PCT = 0.3
HY_SLOW_DECAY_PCT = 1.5
GDN_CHUNK = 128
LN_EPS = 1e-5
RMS_EPS = 1e-6
L2_EPS = 1e-6
DEPTH = 2
DN_ALPHA = (2 * DEPTH) ** 0.25

LANES = 128
SLAB = 128
SLAB_PITCH = SLAB + 8
VMEM_LIMIT = 56 * 1024 * 1024


def _cparams(sem):
    return pltpu.CompilerParams(dimension_semantics=sem, vmem_limit_bytes=VMEM_LIMIT)


def _mm_kernel(a_ref, w_ref, b_ref, o_ref, acc_ref, *, nk):
    part = jnp.dot(a_ref[...].astype(BF16), w_ref[...].astype(BF16), preferred_element_type=F32)
    if nk == 1:
        o_ref[...] = (part + b_ref[...]).astype(o_ref.dtype)
        return
    k = pl.program_id(2)

    @pl.when(k == 0)
    def _():
        acc_ref[...] = part

    @pl.when(k > 0)
    def _():
        acc_ref[...] += part

    @pl.when(k == nk - 1)
    def _():
        o_ref[...] = (acc_ref[...] + b_ref[...]).astype(o_ref.dtype)


def _pick(n, pref):
    if n <= pref:
        return n
    t = (pref // LANES) * LANES
    while t >= LANES:
        if n % t == 0:
            return t
        t -= LANES
    return n


def matmul(a, w, bias=None, out_dtype=F32, tm=1024, tn=1024, tk=4096):
    m, kd = a.shape
    kd2, n = w.shape
    assert kd == kd2
    tm, tn, tk = _pick(m, tm), _pick(n, tn), _pick(kd, tk)
    nk = kd // tk
    if bias is None:
        bias = jnp.zeros((n,), F32)
    bias = bias.reshape(1, n).astype(F32)
    return pl.pallas_call(
        functools.partial(_mm_kernel, nk=nk),
        grid=(m // tm, n // tn, nk),
        in_specs=[pl.BlockSpec((tm, tk), lambda i, j, k: (i, k)),
                  pl.BlockSpec((tk, tn), lambda i, j, k: (k, j)),
                  pl.BlockSpec((1, tn), lambda i, j, k: (0, j))],
        out_specs=pl.BlockSpec((tm, tn), lambda i, j, k: (i, j)),
        out_shape=jax.ShapeDtypeStruct((m, n), out_dtype),
        scratch_shapes=[pltpu.VMEM((tm, tn) if nk > 1 else (8, LANES), F32)],
        compiler_params=_cparams(("parallel", "parallel", "arbitrary")),
        name="matmul",
    )(a, w, bias)


def _real_form(mat):
    return np.block([[mat.real, -mat.imag], [mat.imag, mat.real]])


@functools.lru_cache(maxsize=None)
def _two_stage_consts(n1, n2):
    n = n1 * n2
    h1 = n1 // 2
    a = np.arange(n1)
    b = np.arange(n2)
    ma = np.exp(-2j * np.pi * np.outer(a, a) / n1)
    mb = np.exp(-2j * np.pi * np.outer(b, b) / n2)
    fa_half = _real_form(ma[:, :h1])
    fa_real = np.concatenate([ma.real, ma.imag], axis=0)
    fb = _real_form(mb)
    fbi = _real_form(np.conj(mb))
    fai = _real_form(np.conj(ma)[:h1, :] / n)
    tw = np.exp(-2j * np.pi * b / n)
    tw = np.stack([np.broadcast_to(tw.real[:, None], (n2, LANES)),
                   np.broadcast_to(tw.imag[:, None], (n2, LANES))])
    return dict(fa_half=fa_half, fa_real=fa_real, fb=fb, fbi=fbi, fai=fai, tw=tw)


@functools.lru_cache(maxsize=None)
def _dense_consts(length):
    n = 2 * length
    f = np.arange(n)
    t = np.arange(length)
    fwd = np.exp(-2j * np.pi * np.outer(f, t) / n)
    fwd_full = np.exp(-2j * np.pi * np.outer(f, f) / n)
    inv = np.exp(2j * np.pi * np.outer(t, f) / n) / n
    return dict(fwd=_real_form(fwd),
                fwd_real=np.concatenate([fwd_full.real, fwd_full.imag], axis=0),
                inv=_real_form(inv))


def _cmul(ar, ai, br, bi):
    return ar * br - ai * bi, ar * bi + ai * br


def _twiddle_init(t_ref):
    t_ref[0] = jnp.ones(t_ref.shape[1:], F32)
    t_ref[1] = jnp.zeros(t_ref.shape[1:], F32)


def _twiddle_step(t_ref, tw_ref, tr, ti):
    nr, ni = _cmul(tr, ti, tw_ref[0], tw_ref[1])
    t_ref[0] = nr
    t_ref[1] = ni


def _twiddle_pair(t_ref, tw_ref):
    tr, ti = t_ref[0], t_ref[1]
    return (tr, ti), _cmul(tr, ti, tw_ref[0], tw_ref[1])


def _column_stage(mat, bufr, bufi, rows_in, rows_out, complex_in):
    def column(b):
        xr = bufr[pl.ds(b, rows_in, stride=SLAB_PITCH), :]
        if not complex_in:
            return xr
        return jnp.concatenate([xr, bufi[pl.ds(b, rows_in, stride=SLAB_PITCH), :]], axis=0)

    def body(p, c):
        x = jnp.concatenate([column(2 * p), column(2 * p + 1)], axis=1).astype(BF16)
        y = jnp.dot(mat, x, preferred_element_type=F32)
        for q in range(2):
            yq = y[:, q * LANES:(q + 1) * LANES]
            bufr[pl.ds(2 * p + q, rows_out, stride=SLAB_PITCH), :] = yq[:rows_out]
            bufi[pl.ds(2 * p + q, rows_out, stride=SLAB_PITCH), :] = yq[rows_out:]
        return c
    lax.fori_loop(0, SLAB // 2, body, 0, unroll=8)


def _filter_spectrum_kernel(feat_ref, featb_ref, wf_ref, wb_ref, delta_ref, fa_ref, fb_ref, tw_ref, o_ref,
                            bufr, bufi, t_ref, *, n1, sb):
    j = pl.program_id(1)
    pitch = SLAB_PITCH
    h1 = n1 // 2
    length = h1 * SLAB

    @pl.when(j == 0)
    def _():
        rate = delta_ref[...] * (1.0 / (length - 1))
        pos = lax.broadcasted_iota(jnp.int32, (SLAB, LANES), 0).astype(F32)
        wf = wf_ref[...].astype(BF16)
        wb = wb_ref[...].astype(BF16)

        def load(a, c):
            src = pl.multiple_of(a * SLAB, SLAB)
            m = pos + lax.convert_element_type(a * SLAB, F32)
            hf = jnp.dot(feat_ref[pl.ds(src, SLAB), :].astype(BF16), wf, preferred_element_type=F32)
            hb = jnp.dot(featb_ref[pl.ds(src, SLAB), :].astype(BF16), wb, preferred_element_type=F32)
            bufr[pl.ds(pl.multiple_of(a * pitch, 8), SLAB), :] = hf * jnp.exp(-m * rate)
            bufr[pl.ds(pl.multiple_of((h1 + a) * pitch, 8), SLAB), :] = hb * jnp.exp((m - length) * rate)
            return c
        lax.fori_loop(0, h1, load, 0, unroll=4)
        tap0 =jnp.dot(feat_ref[0:8, :].astype(BF16), wb, preferred_element_type=F32)
        bufr[0:1, :] = bufr[0:1, :] + tap0[0:1]
        _column_stage(fa_ref[...], bufr, bufi, n1, n1, False)
        _twiddle_init(t_ref)

    fb = fb_ref[...]

    def stage_b(s, c):
        twiddles = _twiddle_pair(t_ref, tw_ref)
        cols = []
        for q in range(2):
            row = pl.multiple_of((j * sb + 2 * s + q) * pitch, 8)
            yr, yi = _cmul(bufr[pl.ds(row, SLAB), :], bufi[pl.ds(row, SLAB), :], *twiddles[q])
            cols.append(jnp.concatenate([yr, yi], axis=0))
        spec = jnp.dot(fb, jnp.concatenate(cols, axis=1).astype(BF16), preferred_element_type=F32)
        for q in range(2):
            out = pl.multiple_of((2 * s + q) * SLAB, SLAB)
            o_ref[0, pl.ds(out, SLAB), :] = spec[:SLAB, q * LANES:(q + 1) * LANES]
            o_ref[1, pl.ds(out, SLAB), :] = spec[SLAB:, q * LANES:(q + 1) * LANES]
        _twiddle_step(t_ref, tw_ref, *twiddles[1])
        return c
    lax.fori_loop(0, sb // 2, stage_b, 0, unroll=4)


def _slab_blocks(n1):
    return 8 if n1 % 8 == 0 and n1 >= 16 else 1


def filter_spectrum(feat, feat_circ, w_out, fwd_group, bwd_group, deltas):
    length, fh = feat.shape
    d = deltas.shape[0]
    n = 2 * length
    n1 = n // SLAB
    kb = _slab_blocks(n1)
    sb = n1 // kb
    c = _two_stage_consts(n1, SLAB)
    fa = jnp.asarray(c["fa_real"], BF16)
    fb = jnp.asarray(c["fb"], BF16)
    tw = jnp.asarray(c["tw"], F32)
    const = lambda shape: pl.BlockSpec(shape, lambda i, j: (0,) * len(shape))
    return pl.pallas_call(
        functools.partial(_filter_spectrum_kernel, n1=n1, sb=sb),
        grid=(d // LANES, kb),
        in_specs=[const(feat.shape), const(feat.shape),
                  pl.BlockSpec((fh, LANES), lambda i, j: (0, fwd_group * (d // LANES) + i)),
                  pl.BlockSpec((fh, LANES), lambda i, j: (0, bwd_group * (d // LANES) + i)),
                  pl.BlockSpec((1, LANES), lambda i, j: (0, i)),
                  const(fa.shape), const(fb.shape), const(tw.shape)],
        out_specs=pl.BlockSpec((2, sb * SLAB, LANES), lambda i, j: (0, j, i)),
        out_shape=jax.ShapeDtypeStruct((2, n, d), F32),
        scratch_shapes=[pltpu.VMEM((n1 * SLAB_PITCH, LANES), F32),
                        pltpu.VMEM((n1 * SLAB_PITCH, LANES), F32),
                        pltpu.VMEM((2, SLAB, LANES), F32)],
        compiler_params=_cparams(("parallel", "arbitrary")),
        name="filter_spectrum",
    )(feat, feat_circ, w_out, w_out, deltas.reshape(1, d), fa, fb, tw)


def _fftconv_kernel(z_ref, g_ref, ks_ref, skip_ref, fa_ref, fb_ref, fbi_ref, fai_ref, tw_ref, o_ref,
                    bufr, bufi, t_ref, *, n1, sb, kb):
    j = pl.program_id(1)
    pitch = SLAB_PITCH
    h1 = n1 // 2

    @pl.when(j == 0)
    def _():
        def load(a, c):
            src = pl.multiple_of(a * SLAB, SLAB)
            dst = pl.multiple_of(a * pitch, 8)
            bufr[pl.ds(dst, SLAB), :] = z_ref[0, pl.ds(src, SLAB), :].astype(F32)
            bufi[pl.ds(dst, SLAB), :] = z_ref[1, pl.ds(src, SLAB), :].astype(F32)
            return c
        lax.fori_loop(0, h1, load, 0)
        _column_stage(fa_ref[...], bufr, bufi, h1, n1, True)
        _twiddle_init(t_ref)

    fb = fb_ref[...]
    fbi = fbi_ref[...]

    def stage_b(s, c):
        twiddles = _twiddle_pair(t_ref, tw_ref)
        rows = [pl.multiple_of((j * sb + 2 * s + q) * pitch, 8) for q in range(2)]
        cols = []
        for q in range(2):
            yr, yi = _cmul(bufr[pl.ds(rows[q], SLAB), :], bufi[pl.ds(rows[q], SLAB), :], *twiddles[q])
            cols.append(jnp.concatenate([yr, yi], axis=0))
        spec = jnp.dot(fb, jnp.concatenate(cols, axis=1).astype(BF16), preferred_element_type=F32)
        prods = []
        for q in range(2):
            krow = pl.multiple_of((2 * s + q) * SLAB, SLAB)
            sq = spec[:, q * LANES:(q + 1) * LANES]
            pr, pi = _cmul(sq[:SLAB], sq[SLAB:], ks_ref[0, pl.ds(krow, SLAB), :], ks_ref[1, pl.ds(krow, SLAB), :])
            prods.append(jnp.concatenate([pr, pi], axis=0))
        back = jnp.dot(fbi, jnp.concatenate(prods, axis=1).astype(BF16), preferred_element_type=F32)
        for q in range(2):
            bq = back[:, q * LANES:(q + 1) * LANES]
            ur, ui = _cmul(bq[:SLAB], bq[SLAB:], twiddles[q][0], -twiddles[q][1])
            bufr[pl.ds(rows[q], SLAB), :] = ur
            bufi[pl.ds(rows[q], SLAB), :] = ui
        _twiddle_step(t_ref, tw_ref, *twiddles[1])
        return c
    lax.fori_loop(0, sb // 2, stage_b, 0, unroll=4)

    @pl.when(j == kb - 1)
    def _():
        _column_stage(fai_ref[...], bufr, bufi, n1, h1, True)
        skip = skip_ref[...]

        def store(a, c):
            dst = pl.multiple_of(a * SLAB, SLAB)
            src = pl.multiple_of(a * pitch, 8)
            for bi, buf in enumerate((bufr, bufi)):
                zz = z_ref[bi, pl.ds(dst, SLAB), :].astype(F32)
                gg = g_ref[bi, pl.ds(dst, SLAB), :].astype(F32)
                o_ref[bi, pl.ds(dst, SLAB), :] = (gg * (buf[pl.ds(src, SLAB), :] + zz * skip)).astype(o_ref.dtype)
            return c
        lax.fori_loop(0, h1, store, 0)


def fftconv_gated(z, gate, kspec, skip, z_col=0, gate_col=0, out_dtype=BF16):
    bsz, length, _ = z.shape
    d = skip.shape[0]
    assert bsz == 2
    n1 = 2 * length // SLAB
    kb = _slab_blocks(n1)
    sb = n1 // kb
    c = _two_stage_consts(n1, SLAB)
    fa = jnp.asarray(c["fa_half"], BF16)
    fb = jnp.asarray(c["fb"], BF16)
    fbi = jnp.asarray(c["fbi"], BF16)
    fai = jnp.asarray(c["fai"], BF16)
    tw = jnp.asarray(c["tw"], F32)
    const = lambda shape: pl.BlockSpec(shape, lambda i, j: (0,) * len(shape))
    seq = lambda col=0: pl.BlockSpec((2, length, LANES), lambda i, j: (0, 0, col // LANES + i))
    return pl.pallas_call(
        functools.partial(_fftconv_kernel, n1=n1, sb=sb, kb=kb),
        grid=(d // LANES, kb),
        in_specs=[seq(z_col), seq(gate_col),
                  pl.BlockSpec((2, sb * SLAB, LANES), lambda i, j: (0, j, i)),
                  pl.BlockSpec((1, LANES), lambda i, j: (0, i)),
                  const(fa.shape), const(fb.shape), const(fbi.shape), const(fai.shape), const(tw.shape)],
        out_specs=seq(),
        out_shape=jax.ShapeDtypeStruct((2, length, d), out_dtype),
        scratch_shapes=[pltpu.VMEM((n1 * SLAB_PITCH, LANES), F32),
                        pltpu.VMEM((n1 * SLAB_PITCH, LANES), F32),
                        pltpu.VMEM((2, SLAB, LANES), F32)],
        compiler_params=_cparams(("parallel", "arbitrary")),
        name="fftconv",
    )(z, gate, kspec, skip.reshape(1, d), fa, fb, fbi, fai, tw)


def _dense_conv_kernel(z_ref, g_ref, k_ref, skip_ref, fwd_ref, fwdk_ref, inv_ref, o_ref, *, length):
    n = 2 * length
    x = jnp.concatenate([z_ref[0], z_ref[1]], axis=0)
    spec = jnp.dot(fwd_ref[...], x.astype(BF16), preferred_element_type=F32)
    kspec = jnp.dot(fwdk_ref[...], k_ref[...].astype(BF16), preferred_element_type=F32)
    pr, pi = _cmul(spec[:n], spec[n:], kspec[:n], kspec[n:])
    y = jnp.dot(inv_ref[...], jnp.concatenate([pr, pi], axis=0).astype(BF16), preferred_element_type=F32)
    skip = skip_ref[...]
    for bi in range(2):
        yy = y[bi * length:(bi + 1) * length]
        o_ref[bi] = (g_ref[bi].astype(F32) * (yy + z_ref[bi].astype(F32) * skip)).astype(o_ref.dtype)


def dense_conv_gated(z, gate, k_time, skip, z_col=0, gate_col=0, out_dtype=BF16):
    bsz, length, _ = z.shape
    d = skip.shape[0]
    assert bsz == 2
    c = _dense_consts(length)
    fwd = jnp.asarray(c["fwd"], BF16)
    fwdk = jnp.asarray(c["fwd_real"], BF16)
    inv = jnp.asarray(c["inv"], BF16)
    tile = 2 * LANES if d % (2 * LANES) == 0 else LANES
    const = lambda shape: pl.BlockSpec(shape, lambda i: (0,) * len(shape))
    seq = lambda col=0: pl.BlockSpec((2, length, tile), lambda i: (0, 0, col // tile + i))
    return pl.pallas_call(
        functools.partial(_dense_conv_kernel, length=length),
        grid=(d // tile,),
        in_specs=[seq(z_col), seq(gate_col), pl.BlockSpec((2 * length, tile), lambda i: (0, i)),
                  pl.BlockSpec((1, tile), lambda i: (0, i)),
                  const(fwd.shape), const(fwdk.shape), const(inv.shape)],
        out_specs=seq(),
        out_shape=jax.ShapeDtypeStruct((2, length, d), out_dtype),
        compiler_params=_cparams(("parallel",)),
        name="dense_conv",
    )(z, gate, k_time, skip.reshape(1, d), fwd, fwdk, inv)


def _bdot(a, b):
    return lax.dot_general(a.astype(BF16), b.astype(BF16), (((2,), (1,)), ((0,), (0,))),
                           preferred_element_type=F32)


def _bdot_nt(a, b):
    return lax.dot_general(a.astype(BF16), b.astype(BF16), (((2,), (2,)), ((0,), (0,))),
                           preferred_element_type=F32)


def _dot(a, b):
    return jnp.dot(a.astype(BF16), b.astype(BF16), preferred_element_type=F32)


def _dot_tn(a, b):
    return lax.dot_general(a.astype(BF16), b.astype(BF16), (((0,), (0,)), ((), ())), preferred_element_type=F32)


def _gdn_kernel(qf_ref, kf_ref, vf_ref, qb_ref, kb_ref, vb_ref, gcf_ref, bcf_ref, gcb_ref, bcb_ref,
                grf_ref, grb_ref, s0_ref, of_ref, ob_ref, sout_ref,
                s_ref, u_ref, w_ref, qg_ref, at_ref, gl_ref, mw_ref, n_ref, sc_ref, *, r, nb, hb, nh):
    cs = GDN_CHUNK
    head0 = pl.program_id(1) * hb
    j = pl.program_id(2)

    @pl.when(j == 0)
    def _():
        s_ref[...] = s0_ref[0]

    ii = lax.broadcasted_iota(jnp.int32, (cs, cs), 0)
    jj = lax.broadcasted_iota(jnp.int32, (cs, cs), 1)
    eye = (ii == jj).astype(F32)
    blk_same = [jnp.right_shift(ii, lvl) == jnp.right_shift(jj, lvl) for lvl in range(int(math.log2(cs)) + 1)]
    lane_head = lax.broadcasted_iota(jnp.int32, (r * cs, nh), 1)

    for dr in range(2):
        q_ref, k_ref, v_ref = (qf_ref, kf_ref, vf_ref) if dr == 0 else (qb_ref, kb_ref, vb_ref)
        gcol_ref, bcol_ref = (gcf_ref, bcf_ref) if dr == 0 else (gcb_ref, bcb_ref)
        grow_ref = grf_ref if dr == 0 else grb_ref
        incl = (ii >= jj) if dr == 0 else (ii <= jj)
        strict = (ii > jj) if dr == 0 else (ii < jj)
        last = cs - 1 if dr == 0 else 0
        def heads(ref):
            return jnp.concatenate([ref[0, :, hh * LANES:(hh + 1) * LANES].astype(F32).reshape(r, cs, LANES)
                                    for hh in range(hb)], axis=0)

        def head_cols(ref):
            return jnp.concatenate(
                [jnp.sum(jnp.where(lane_head == head0 + hh, ref[0, 0], 0.0), axis=1, keepdims=True).reshape(r, cs, 1)
                 for hh in range(hb)], axis=0)
        q, k, v = heads(q_ref), heads(k_ref), heads(v_ref)
        gc, bc = head_cols(gcol_ref), head_cols(bcol_ref)
        gr = grow_ref[0, 0].reshape(hb * r, 1, cs)
        g_last = gr[:, :, last:last + 1]
        decay = jnp.where(incl, jnp.exp(jnp.where(incl, gc - gr, 0.0)), 0.0)
        kbeta = k * bc
        a_mat = jnp.where(strict, _bdot_nt(kbeta, k) * decay, 0.0).astype(BF16).astype(F32)
        t = eye - jnp.where(blk_same[1] & ~blk_same[0], a_mat, 0.0)
        for lvl in range(1, len(blk_same) - 1):
            x = jnp.where(blk_same[lvl + 1] & ~blk_same[lvl], a_mat, 0.0)
            t = t - _bdot(t, _bdot(x, t))
        eg = jnp.exp(gc)
        sol = _bdot(t, jnp.concatenate([v * bc, kbeta * eg], axis=2))
        u, w = sol[:, :, :LANES], sol[:, :, LANES:]
        per_head = lambda a: a.reshape((hb, r) + a.shape[1:])
        u_ref[dr] = per_head(u)
        w_ref[dr] = per_head(w)
        at_ref[dr] = per_head(jnp.where(incl, _bdot_nt(q, k) * decay, 0.0))
        qg_ref[dr] = per_head(q * eg)
        kg_t = jnp.swapaxes(k * jnp.exp(g_last - gc), 1, 2)
        mw_ref[dr] = per_head(-_bdot(kg_t, w))
        n_ref[dr] = per_head(_bdot(kg_t, u))
        gl_ref[dr] = per_head(jnp.broadcast_to(jnp.exp(g_last), (hb * r, 8, LANES)))

    def scan(p, carry):
        for dr in range(2):
            c = p if dr == 0 else r - 1 - p
            for hh in range(hb):
                s = s_ref[dr, hh]
                sc_ref[dr, hh, c] = s
                s_ref[dr, hh] = s * gl_ref[dr, hh, c][0:1, :] + _dot(mw_ref[dr, hh, c], s) + n_ref[dr, hh, c]
        return carry
    lax.fori_loop(0, r, scan, 0, unroll=2)

    for dr in range(2):
        o_ref = of_ref if dr == 0 else ob_ref
        flat = lambda ref: ref[dr].reshape((hb * r,) + ref.shape[3:])
        sc = flat(sc_ref)
        v_new = flat(u_ref) - _bdot(flat(w_ref), sc)
        o = _bdot(flat(qg_ref), sc) + _bdot(flat(at_ref), v_new)
        for hh in range(hb):
            o_ref[0, :, hh * LANES:(hh + 1) * LANES] = (
                o[hh * r:(hh + 1) * r].reshape(r * cs, LANES).astype(o_ref.dtype))

    @pl.when(j == nb - 1)
    def _():
        sout_ref[0] = s_ref[...]


def gdn_scan(qkv, gcol, bcol, grow, s0, nh):
    bsz, length, d3 = qkv.shape
    d = d3 // 3
    dh = d // nh
    assert dh == LANES
    hb = 4 if nh % 4 == 0 else 2 if nh % 2 == 0 else 1
    ng = nh // hb
    nch = length // GDN_CHUNK
    r = 2 * ROW_TILE // GDN_CHUNK
    r = r if nch % r == 0 else nch
    nb = nch // r
    rows = r * GDN_CHUNK
    blk = lambda j, rev: nb - 1 - j if rev else j
    seq = lambda off, rev: pl.BlockSpec((1, rows, hb * LANES), lambda b, g, j: (b, blk(j, rev), off * ng + g))
    col = lambda rev: pl.BlockSpec((1, 1, rows, nh), lambda b, g, j: (b, rev, blk(j, rev), 0))
    row = lambda rev: pl.BlockSpec((1, 1, hb, r, 1, GDN_CHUNK), lambda b, g, j: (b, rev, g, blk(j, rev), 0, 0))
    out = lambda rev: pl.BlockSpec((1, rows, hb * LANES), lambda b, g, j: (b, blk(j, rev), g))
    state = pl.BlockSpec((1, 2, hb, dh, dh), lambda b, g, j: (b, 0, g, 0, 0))
    per_chunk = lambda *shape: pltpu.VMEM((2, hb, r) + shape, F32)
    return pl.pallas_call(
        functools.partial(_gdn_kernel, r=r, nb=nb, hb=hb, nh=nh),
        grid=(bsz, ng, nb),
        in_specs=[seq(0, 0), seq(1, 0), seq(2, 0), seq(0, 1), seq(1, 1), seq(2, 1),
                  col(0), col(0), col(1), col(1), row(0), row(1), state],
        out_specs=[out(0), out(1), state],
        out_shape=[jax.ShapeDtypeStruct((bsz, length, d), BF16),
                   jax.ShapeDtypeStruct((bsz, length, d), BF16),
                   jax.ShapeDtypeStruct((bsz, 2, nh, dh, dh), F32)],
        scratch_shapes=[pltpu.VMEM((2, hb, dh, dh), F32),
                        per_chunk(GDN_CHUNK, LANES), per_chunk(GDN_CHUNK, LANES), per_chunk(GDN_CHUNK, LANES),
                        per_chunk(GDN_CHUNK, GDN_CHUNK), per_chunk(8, LANES),
                        per_chunk(dh, dh), per_chunk(dh, dh), per_chunk(dh, dh)],
        compiler_params=_cparams(("parallel", "parallel", "arbitrary")),
        name="gdn_scan",
    )(qkv, qkv, qkv, qkv, qkv, qkv, gcol, bcol, gcol, bcol, grow, grow, s0)


ROW_TILE = 256
HALO = 16


def _per_batch(v, bsz):
    return jnp.broadcast_to(v.reshape(-1, 1, v.shape[-1]), (bsz, 1, v.shape[-1]))


def _modulate_kernel(x_ref, sh_ref, sc_ref, h_ref):
    h_ref[0] = (x_ref[0] * (1.0 + sc_ref[0]) + sh_ref[0]).astype(h_ref.dtype)


def modulate(x, shift, scale):
    bsz, length, d = x.shape
    tr = _pick(length, ROW_TILE)
    row = pl.BlockSpec((1, tr, d), lambda b, i: (b, i, 0))
    vec = pl.BlockSpec((1, 1, d), lambda b, i: (b, 0, 0))
    return pl.pallas_call(
        _modulate_kernel, grid=(bsz, length // tr), in_specs=[row, vec, vec], out_specs=row,
        out_shape=jax.ShapeDtypeStruct(x.shape, BF16), compiler_params=_cparams(("parallel", "parallel")),
        name="modulate",
    )(x, _per_batch(shift, bsz), _per_batch(scale, bsz))


def _post_norm_kernel(x_ref, y_ref, gate_ref, g_ref, b_ref, sh_ref, sc_ref, xo_ref, *h_ref):
    v = DN_ALPHA * x_ref[0] + gate_ref[0] * y_ref[0].astype(F32)
    cen = v - jnp.mean(v, axis=1, keepdims=True)
    out = cen * lax.rsqrt(jnp.mean(cen * cen, axis=1, keepdims=True) + LN_EPS) * g_ref[...] + b_ref[...]
    xo_ref[0] = out
    if h_ref:
        h_ref[0][0] = (out * (1.0 + sc_ref[0]) + sh_ref[0]).astype(BF16)


def post_norm(x, y, gate, g, b, shift=None, scale=None):
    bsz, length, d = x.shape
    with_h = shift is not None
    if not with_h:
        shift = scale = jnp.zeros((1, d), F32)
    tr = _pick(length, ROW_TILE)
    row = pl.BlockSpec((1, tr, d), lambda b, i: (b, i, 0))
    vec = pl.BlockSpec((1, 1, d), lambda b, i: (b, 0, 0))
    par = pl.BlockSpec((1, d), lambda b, i: (0, 0))
    out = pl.pallas_call(
        _post_norm_kernel, grid=(bsz, length // tr),
        in_specs=[row, row, vec, par, par, vec, vec],
        out_specs=[row, row] if with_h else [row],
        out_shape=[jax.ShapeDtypeStruct(x.shape, F32)] + ([jax.ShapeDtypeStruct(x.shape, BF16)] if with_h else []),
        compiler_params=_cparams(("parallel", "parallel")), name="post_norm",
    )(x, y, _per_batch(gate, bsz), g.reshape(1, d), b.reshape(1, d), _per_batch(shift, bsz), _per_batch(scale, bsz))
    return out if with_h else (out[0], None)


def _seq_neighbours(x, prev_ref, next_ref, n_steps):
    i = pl.program_id(2)
    rows = x.shape[0]
    row = lax.broadcasted_iota(jnp.int32, x.shape, 0)
    before = jnp.where(i == 0, 0.0, prev_ref[0, HALO - 1:HALO, :].astype(F32))
    after = jnp.where(i == n_steps - 1, 0.0, next_ref[0, 0:1, :].astype(F32))
    prev = jnp.where(row == 0, before, pltpu.roll(x, 1, 0))
    nxt = jnp.where(row == rows - 1, after, pltpu.roll(x, rows - 1, 0))
    return prev, nxt


def _seq_specs(tr, tc, length):
    per = tr // HALO
    last = length // HALO - 1
    main = pl.BlockSpec((1, tr, tc), lambda b, j, i: (b, i, j))
    prev = pl.BlockSpec((1, HALO, tc), lambda b, j, i: (b, jnp.maximum(i * per - 1, 0), j))
    nxt = pl.BlockSpec((1, HALO, tc), lambda b, j, i: (b, jnp.minimum((i + 1) * per, last), j))
    return main, prev, nxt


def _hy_short_conv_kernel(x_ref, p_ref, n_ref, w_ref, b_ref, o_ref, *, n_steps):
    x = x_ref[0].astype(F32)
    prev, nxt = _seq_neighbours(x, p_ref, n_ref, n_steps)
    o_ref[0] = (prev * w_ref[0:1, :] + x * w_ref[1:2, :] + nxt * w_ref[2:3, :] + b_ref[...]).astype(o_ref.dtype)


def hy_short_conv(x, w, b):
    bsz, length, ch = x.shape
    tr, tc = _pick(length, 2 * ROW_TILE), _pick(ch, 1024)
    main, prev, nxt = _seq_specs(tr, tc, length)
    return pl.pallas_call(
        functools.partial(_hy_short_conv_kernel, n_steps=length // tr),
        grid=(bsz, ch // tc, length // tr),
        in_specs=[main, prev, nxt, pl.BlockSpec((3, tc), lambda b_, j, i: (0, j)),
                  pl.BlockSpec((1, tc), lambda b_, j, i: (0, j))],
        out_specs=main, out_shape=jax.ShapeDtypeStruct(x.shape, BF16),
        compiler_params=_cparams(("parallel", "parallel", "parallel")), name="hy_short_conv",
    )(x, x, x, w, b.reshape(1, ch))


def _gdn_pre_kernel(x_ref, p_ref, n_ref, w_ref, o_ref, *, n_steps, q_blocks, qk_blocks, q_scale):
    x = x_ref[0].astype(F32)
    prev, nxt = _seq_neighbours(x, p_ref, n_ref, n_steps)
    y = prev * w_ref[0:1, :] + x * w_ref[1:2, :] + nxt * w_ref[2:3, :]
    y = y * jax.nn.sigmoid(y)
    j = pl.program_id(1)
    scale = jnp.where(j < q_blocks, q_scale, 1.0)
    for h in range(y.shape[1] // LANES):
        lanes = slice(h * LANES, (h + 1) * LANES)
        yh = y[:, lanes]
        normed = yh * (lax.rsqrt(jnp.sum(yh * yh, axis=1, keepdims=True) + L2_EPS) * scale)
        o_ref[0, :, lanes] = jnp.where(j < qk_blocks, normed, yh).astype(o_ref.dtype)


def gdn_pre(proj, w_conv, d, dh):
    bsz, length, _ = proj.shape
    tr, tc = _pick(length, 2 * ROW_TILE), _pick(d, 1024)
    main, prev, nxt = _seq_specs(tr, tc, length)
    return pl.pallas_call(
        functools.partial(_gdn_pre_kernel, n_steps=length // tr, q_blocks=d // tc, qk_blocks=2 * d // tc,
                          q_scale=dh ** -0.5),
        grid=(bsz, 3 * d // tc, length // tr),
        in_specs=[main, prev, nxt, pl.BlockSpec((3, tc), lambda b_, j, i: (0, j))],
        out_specs=main, out_shape=jax.ShapeDtypeStruct((bsz, length, 3 * d), BF16),
        compiler_params=_cparams(("parallel", "parallel", "parallel")), name="gdn_pre",
    )(proj, proj, proj, w_conv)


def _gdn_post_kernel(of_ref, ob_ref, z_ref, nw_ref, o_ref):
    o = of_ref[0].astype(F32) + ob_ref[0].astype(F32)
    z = z_ref[0].astype(F32)
    gate = z * jax.nn.sigmoid(z) * nw_ref[...]
    for h in range(o.shape[1] // LANES):
        lanes = slice(h * LANES, (h + 1) * LANES)
        oh = o[:, lanes]
        rms = lax.rsqrt(jnp.mean(oh * oh, axis=1, keepdims=True) + RMS_EPS)
        o_ref[0, :, lanes] = (oh * rms * gate[:, lanes]).astype(o_ref.dtype)


def gdn_post(o_f, o_b, proj, norm_w):
    bsz, length, d = o_f.shape
    tr, tc = _pick(length, 2 * ROW_TILE), _pick(d, 512)
    z_off = (proj.shape[2] - d) // tc
    row = pl.BlockSpec((1, tr, tc), lambda b, j, i: (b, i, j))
    nw = jnp.tile(norm_w, tc // norm_w.shape[0]).reshape(1, tc)
    return pl.pallas_call(
        _gdn_post_kernel, grid=(bsz, d // tc, length // tr),
        in_specs=[row, row, pl.BlockSpec((1, tr, tc), lambda b, j, i: (b, i, z_off + j)),
                  pl.BlockSpec((1, tc), lambda b, j, i: (0, 0))],
        out_specs=row, out_shape=jax.ShapeDtypeStruct(o_f.shape, BF16),
        compiler_params=_cparams(("parallel", "parallel", "parallel")), name="gdn_post",
    )(o_f, o_b, proj, nw)


def _ffn_act_kernel(g_ref, v_ref, up_ref, dn_ref, w_ref, b_ref, sl_ref, sr_ref, o_ref, xs_ref, *, n_steps, gw, blk):
    i = pl.program_id(2)
    rows = g_ref.shape[1]
    xs_ref[0:gw] = jnp.where(i == 0, jnp.zeros_like(up_ref[0]), up_ref[0])
    xs_ref[gw:gw + rows] = g_ref[0]
    xs_ref[gw + rows:] = jnp.where(i == n_steps - 1, jnp.zeros_like(dn_ref[0]), dn_ref[0])
    for r0 in range(0, rows, blk):
        xw = xs_ref[r0:r0 + blk + 2 * gw]
        left = jnp.dot(sl_ref[...], xw, preferred_element_type=F32)
        right = jnp.dot(sr_ref[...], xw, preferred_element_type=F32)
        x = xw.astype(F32)
        acc = b_ref[...]
        for di in range(3):
            rs = slice(di * gw, di * gw + blk)
            acc = acc + (left[rs] * w_ref[3 * di:3 * di + 1, :] + x[rs] * w_ref[3 * di + 1:3 * di + 2, :]
                         + right[rs] * w_ref[3 * di + 2:3 * di + 3, :])
        gelu = 0.5 * acc * (1.0 + lax.erf(acc * (2.0 ** -0.5)))
        o_ref[0, r0:r0 + blk] = (gelu * v_ref[0, r0:r0 + blk].astype(F32)).astype(o_ref.dtype)


def ffn_act(up, w_dw, b_dw, gw):
    bsz, length, f2 = up.shape
    f = f2 // 2
    tc = _pick(f, 1024)
    tr = gw * max(1, min(length // gw, 4 * ROW_TILE // gw))
    per = tr // gw
    last = length // gw - 1
    main = lambda off: pl.BlockSpec((1, tr, tc), lambda b, j, i: (b, i, off + j))
    halo_up = pl.BlockSpec((1, gw, tc), lambda b, j, i: (b, jnp.maximum(i * per - 1, 0), j))
    halo_dn = pl.BlockSpec((1, gw, tc), lambda b, j, i: (b, jnp.minimum((i + 1) * per, last), j))
    blk = max(gw, LANES)
    assert blk % gw == 0 and tr % blk == 0
    win = blk + 2 * gw
    pos = np.arange(win)
    s_left = np.zeros((win, win), np.float32)
    s_left[pos[pos % gw != 0], pos[pos % gw != 0] - 1] = 1.0
    s_right = np.zeros((win, win), np.float32)
    s_right[pos[pos % gw != gw - 1], pos[pos % gw != gw - 1] + 1] = 1.0
    const = pl.BlockSpec((win, win), lambda b, j, i: (0, 0))
    return pl.pallas_call(
        functools.partial(_ffn_act_kernel, n_steps=length // tr, gw=gw, blk=blk),
        grid=(bsz, f // tc, length // tr),
        in_specs=[main(0), main(f // tc), halo_up, halo_dn,
                  pl.BlockSpec((9, tc), lambda b, j, i: (0, j)), pl.BlockSpec((1, tc), lambda b, j, i: (0, j)),
                  const, const],
        out_specs=main(0), out_shape=jax.ShapeDtypeStruct((bsz, length, f), BF16),
        scratch_shapes=[pltpu.VMEM((tr + 2 * gw, tc), BF16)],
        compiler_params=_cparams(("parallel", "parallel", "parallel")), name="ffn_act",
    )(up, up, up, up, w_dw.reshape(9, f), b_dw.reshape(1, f), jnp.asarray(s_left, BF16), jnp.asarray(s_right, BF16))


def _proj(h, w, bias=None, out_dtype=F32, **tiles):
    bsz, length, kd = h.shape
    return matmul(h.reshape(bsz * length, kd).astype(BF16), w, bias, out_dtype, **tiles).reshape(bsz, length, -1)


def _hyena_filter_features(length, p, d):
    t = jnp.linspace(0.0, 1.0, length, dtype=F32)[:, None]
    bands = (HY_EMB_DIM - 1) // 2
    ang = 2.0 * math.pi * jnp.arange(length, dtype=F32)[:, None] / length
    f = jnp.linspace(1e-4, bands - 1, bands, dtype=F32)[None, :]
    z = jnp.concatenate([t, jnp.cos(f * ang), -jnp.sin(f * ang)], axis=-1)
    freq = p['hy_f_freq']
    h = jnp.sin(freq * (z @ p['hy_f_w1'] + p['hy_f_b1']))
    h = jnp.sin(freq * (h @ p['hy_f_w2'] + p['hy_f_b2']))
    h = jnp.sin(freq * (h @ p['hy_f_w3'] + p['hy_f_b3']))
    max_decay = math.log(HY_DECAY_TARGET) / HY_FAST_DECAY_PCT
    min_decay = math.log(HY_DECAY_TARGET) / HY_SLOW_DECAY_PCT
    deltas = jnp.abs(jnp.linspace(min_decay, max_decay, d, dtype=F32))
    return h, jnp.concatenate([jnp.zeros_like(h[:1]), h[:0:-1]], axis=0), deltas


def _windowed_filter(feat, feat_circ, w_fwd, w_bwd, deltas):
    length = feat.shape[0]
    m = jnp.arange(length, dtype=F32)[:, None]
    rate = deltas / (length - 1)
    head = matmul(feat, w_fwd) * jnp.exp(-m * rate)
    tail = matmul(feat_circ, w_bwd) * jnp.exp((m - length) * rate)
    return jnp.concatenate([head[:1] + matmul(feat[:8], w_bwd)[:1], head[1:], tail], axis=0)


def _hyena_mixer(h, p, w_in, w_out):
    bsz, length, d = h.shape
    u = hy_short_conv(_proj(h, w_in, p['hy_b_in'], out_dtype=BF16), p['hy_w_short'], p['hy_b_short'])
    feat, feat_circ, deltas = _hyena_filter_features(length, p, d)
    w_filt = p['hy_f_wout']
    z, z_col = u, HY_ORDER * d
    two_stage = (2 * length) % (16 * SLAB) == 0
    for o in range(HY_ORDER):
        if two_stage:
            kspec = filter_spectrum(feat, feat_circ, w_filt, 2 * o, 2 * o + 1, deltas)
            z = fftconv_gated(z, u, kspec, p['hy_skip'][o], z_col, o * d)
        else:
            k_time = _windowed_filter(feat, feat_circ, w_filt[:, 2 * o * d:(2 * o + 1) * d],
                                      w_filt[:, (2 * o + 1) * d:(2 * o + 2) * d], deltas)
            z = dense_conv_gated(z, u, k_time, p['hy_skip'][o], z_col, o * d)
        z_col = 0
    return _proj(z, w_out, p['hy_b_out'], out_dtype=BF16)


def _gdn_mixer(h, p, w_qkvz, w_ab, w_out, s0):
    bsz, length, d = h.shape
    nh = p['gdn_a_log'].shape[1]
    dh = d // nh
    proj = _proj(h, w_qkvz, out_dtype=BF16)
    ab = _proj(h, w_ab).reshape(bsz, length, 2, 2, nh)
    qkv = gdn_pre(proj, p['gdn_w_conv'], d, dh)
    g =-jnp.exp(p['gdn_a_log']) * jax.nn.softplus(ab[:, :, 0] + p['gdn_dt_bias'])
    beta = jax.nn.sigmoid(ab[:, :, 1]).transpose(0, 2, 1, 3)
    nch = length // GDN_CHUNK
    g = g.transpose(0, 2, 1, 3).reshape(bsz, 2, nch, GDN_CHUNK, nh)
    g = jnp.stack([jnp.cumsum(g[:, 0], axis=2),
                   jnp.flip(jnp.cumsum(jnp.flip(g[:, 1], axis=2), axis=2), axis=2)], axis=1)
    gcol = g.reshape(bsz, 2, length, nh)
    grow = g.transpose(0, 1, 4, 2, 3)[:, :, :, :, None, :]
    o_f, o_b, s_out = gdn_scan(qkv, gcol, beta, grow, s0, nh)
    return _proj(gdn_post(o_f, o_b, proj, p['gdn_norm_w']), w_out, out_dtype=BF16), s_out


FFN_PAD = 1024


def _ffn_weights(p):
    f = p['ffn_b_dw'].shape[0]
    fp = -(-f // FFN_PAD) * FFN_PAD if f > FFN_PAD else f
    pad = lambda a, axis: jnp.pad(a, [(0, fp - f) if ax == axis else (0, 0) for ax in range(a.ndim)])
    w_up = p['ffn_w_up'].astype(BF16)
    w_up = jnp.concatenate([pad(w_up[:, :f], 1), pad(w_up[:, f:], 1)], axis=1)
    return w_up, pad(p['ffn_w_down'].astype(BF16), 0), pad(p['ffn_w_dw'], 2), pad(p['ffn_b_dw'], 0)


def _conv_glu(h, ffn_w, rows, cols):
    w_up, w_down, w_dw, b_dw = ffn_w
    assert h.shape[1] == rows * cols
    return _proj(ffn_act(_proj(h, w_up, out_dtype=BF16), w_dw, b_dw, cols), w_down, out_dtype=BF16)


def kernel(x, c, ctx, c_ctx, l0_w_ada, l0_b_ada, l0_ln1_g, l0_ln1_b, l0_ln2_g, l0_ln2_b, l0_hy_w_in, l0_hy_b_in, l0_hy_w_short, l0_hy_b_short, l0_hy_f_w1, l0_hy_f_b1, l0_hy_f_w2, l0_hy_f_b2, l0_hy_f_w3, l0_hy_f_b3, l0_hy_f_wout, l0_hy_f_freq, l0_hy_skip, l0_hy_w_out, l0_hy_b_out, l0_ffn_w_up, l0_ffn_w_dw, l0_ffn_b_dw, l0_ffn_w_down, l1_w_ada, l1_b_ada, l1_ln1_g, l1_ln1_b, l1_ln2_g, l1_ln2_b, l1_gdn_w_in, l1_gdn_w_conv, l1_gdn_a_log, l1_gdn_dt_bias, l1_gdn_norm_w, l1_gdn_w_out, l1_ffn_w_up, l1_ffn_w_dw, l1_ffn_b_dw, l1_ffn_w_down):
    layers = (
        dict(w_ada=l0_w_ada, b_ada=l0_b_ada, ln1_g=l0_ln1_g, ln1_b=l0_ln1_b, ln2_g=l0_ln2_g, ln2_b=l0_ln2_b,
             hy_w_in=l0_hy_w_in, hy_b_in=l0_hy_b_in, hy_w_short=l0_hy_w_short, hy_b_short=l0_hy_b_short,
             hy_f_w1=l0_hy_f_w1, hy_f_b1=l0_hy_f_b1, hy_f_w2=l0_hy_f_w2, hy_f_b2=l0_hy_f_b2,
             hy_f_w3=l0_hy_f_w3, hy_f_b3=l0_hy_f_b3, hy_f_wout=l0_hy_f_wout, hy_f_freq=l0_hy_f_freq,
             hy_skip=l0_hy_skip, hy_w_out=l0_hy_w_out, hy_b_out=l0_hy_b_out,
             ffn_w_up=l0_ffn_w_up, ffn_w_dw=l0_ffn_w_dw, ffn_b_dw=l0_ffn_b_dw, ffn_w_down=l0_ffn_w_down),
        dict(w_ada=l1_w_ada, b_ada=l1_b_ada, ln1_g=l1_ln1_g, ln1_b=l1_ln1_b, ln2_g=l1_ln2_g, ln2_b=l1_ln2_b,
             gdn_w_in=l1_gdn_w_in, gdn_w_conv=l1_gdn_w_conv, gdn_a_log=l1_gdn_a_log,
             gdn_dt_bias=l1_gdn_dt_bias, gdn_norm_w=l1_gdn_norm_w, gdn_w_out=l1_gdn_w_out,
             ffn_w_up=l1_ffn_w_up, ffn_w_dw=l1_ffn_w_dw, ffn_b_dw=l1_ffn_b_dw, ffn_w_down=l1_ffn_w_down),
    )
    bsz, seq, d = x.shape
    rows = seq // GRID_W
    ctx_len = ctx.shape[1]
    cond = jnp.concatenate([c, c_ctx[None], jnp.zeros((8 - bsz - 1, d), F32)], axis=0)
    mods = []
    for p in layers:
        mod = matmul(jax.nn.silu(cond), p['w_ada'], p['b_ada'], tn=512)
        mods.append((jnp.split(mod[:bsz, None, :], 6, axis=-1),
                     jnp.split(mod[bsz:bsz + 1], 6, axis=-1)))
    h_lat = modulate(x, mods[0][0][0], mods[0][0][1])
    h_ctx = modulate(ctx, mods[0][1][0], mods[0][1][1])
    for i, p in enumerate(layers):
        last = i == DEPTH - 1
        (sh1, sc1, gt1, sh2, sc2, gt2), (csh1, csc1, cgt1, csh2, csc2, cgt2) = mods[i]
        nxt, cnxt = (None, None) if last else (mods[i + 1][0][:2], mods[i + 1][1][:2])
        if 'hy_w_in' in p:
            w_in, w_out = p['hy_w_in'].astype(BF16), p['hy_w_out'].astype(BF16)
            y_lat = _hyena_mixer(h_lat, p, w_in, w_out)
            y_ctx = None if last else _hyena_mixer(h_ctx, p, w_in, w_out)
        else:
            nh = p['gdn_a_log'].shape[1]
            w_qkvz = p['gdn_w_in'][:, :4 * d].astype(BF16)
            w_ab = p['gdn_w_in'][:, 4 * d:].astype(BF16)
            w_out = p['gdn_w_out'].astype(BF16)
            s0 = jnp.zeros((bsz, 2, nh, d // nh, d // nh), F32)
            y_ctx, s_ctx = _gdn_mixer(h_ctx, p, w_qkvz, w_ab, w_out, s0)
            y_lat, _ = _gdn_mixer(h_lat, p, w_qkvz, w_ab, w_out, s_ctx)
        ffn_w = _ffn_weights(p)
        x, h_mid = post_norm(x, y_lat, gt1, p['ln1_g'], p['ln1_b'], sh2, sc2)
        ffn = _conv_glu(h_mid, ffn_w, rows, GRID_W)
        x, h_lat = post_norm(x, ffn, gt2, p['ln2_g'], p['ln2_b'], *(nxt or ()))
        if not last:
            ctx, h_mid = post_norm(ctx, y_ctx, cgt1, p['ln1_g'], p['ln1_b'], csh2, csc2)
            ffn = _conv_glu(h_mid, ffn_w, 1, ctx_len)
            ctx, h_ctx = post_norm(ctx, ffn, cgt2, p['ln2_g'], p['ln2_b'], *cnxt)
    return x
```
